```python
import jax, jax.numpy as jnp
from jax import lax
import numpy as np

D_MODEL = 2048
BATCH = 16
SEQ = 256
DEPTH = 2
DEC_BATCH = 4
DEC_SEQ = 1024
PAST_LEN = 512

GRID_W = 64
HEAD_DIM = 128
MLA_HEADS = 4
MLA_Q_RANK = 384
MLA_KV_RANK = 256
MLA_NOPE = 128
MLA_ROPE = 64
MLA_V = 128
NA_HEADS = 4
NA_ROWS = 8
NA_COLS = 16
GQA_HEADS = 4
GQA_KV_HEADS = 2
SWA_HEADS = 4
SWA_KV_HEADS = 2
SWA_WINDOW = 128
Q_BLOCK = 128
ROPE_THETA = 10000.0
PEER_HEADS = 8
PEER_N_KEYS = 128
PEER_N_EXPERTS = PEER_N_KEYS * PEER_N_KEYS
PEER_D_KEY = 256
PEER_TOPK = 16
PEER_TOKEN_BLOCK = 128
N_MOD = 6
EPS = 1e-6
NEG_INF = -1e30
D_MIX = MLA_HEADS * MLA_V + NA_HEADS * HEAD_DIM + GQA_HEADS * HEAD_DIM + SWA_HEADS * HEAD_DIM
IN_SPLITS = (MLA_Q_RANK, MLA_KV_RANK, MLA_ROPE,
             NA_HEADS * HEAD_DIM, NA_HEADS * HEAD_DIM, NA_HEADS * HEAD_DIM,
             GQA_HEADS * HEAD_DIM, GQA_KV_HEADS * HEAD_DIM, GQA_KV_HEADS * HEAD_DIM,
             SWA_HEADS * HEAD_DIM, SWA_KV_HEADS * HEAD_DIM, SWA_KV_HEADS * HEAD_DIM)
D_IN = sum(IN_SPLITS)

kernel_name = "hybrid_diffusion_parallel_heads_peer"


def rms_norm(x, g):
    xf = x.astype(jnp.float32)
    y = xf * lax.rsqrt(jnp.mean(jnp.square(xf), axis=-1, keepdims=True) + EPS)
    return (y * g.astype(jnp.float32)).astype(x.dtype)


def modulation(cond, w_mod, b_mod):
    m = jax.nn.silu(cond) @ w_mod + b_mod
    return jnp.split(m[:, None, :], N_MOD, axis=-1)


def grid_positions(n_tokens):
    t = jnp.arange(n_tokens)
    return (t // GRID_W).astype(jnp.float32), (t % GRID_W).astype(jnp.float32)


def axial_rope(x, rows, cols):
    d = x.shape[-1]
    n = d // 4
    freqs = ROPE_THETA ** (-jnp.arange(n, dtype=jnp.float32) / n)
    ang = jnp.concatenate([rows[:, None] * freqs, cols[:, None] * freqs], axis=-1)
    cos = jnp.cos(ang)[None, :, None, :]
    sin = jnp.sin(ang)[None, :, None, :]
    x1, x2 = jnp.split(x.astype(jnp.float32), 2, axis=-1)
    return jnp.concatenate([x1 * cos - x2 * sin, x2 * cos + x1 * sin], axis=-1).astype(x.dtype)


def dense_attention(q, k, v, sink=None):
    B, S, H, dq = q.shape
    G, dv = k.shape[2], v.shape[-1]
    rep = H // G
    nb = S // Q_BLOCK
    scale = dq ** -0.5
    qb = q.reshape(B, nb, Q_BLOCK, G, rep, dq).transpose(1, 0, 2, 3, 4, 5)

    def one_block(qi):
        s = jnp.einsum("bqgrd,bkgd->bgrqk", qi, k, preferred_element_type=jnp.float32) * scale
        if sink is None:
            p = jax.nn.softmax(s, axis=-1)
        else:
            s_sink = jnp.broadcast_to(sink.astype(jnp.float32).reshape(1, G, rep, 1, 1), s.shape[:-1] + (1,))
            p = jax.nn.softmax(jnp.concatenate([s, s_sink], axis=-1), axis=-1)[..., :-1]
        o = jnp.einsum("bgrqk,bkgd->bqgrd", p.astype(v.dtype), v)
        return o.reshape(B, Q_BLOCK, H, dv)

    o = lax.map(one_block, qb)
    return o.transpose(1, 0, 2, 3, 4).reshape(B, S, H, dv)


def neighborhood_attention(q, k, v, kc, vc, rpb):
    B, S, H, d = q.shape
    rows = S // GRID_W
    kr = min(NA_ROWS, rows)
    ncb = GRID_W // NA_COLS
    kcw = 2 * NA_COLS
    nk = kr * kcw
    r = jnp.arange(rows)
    rs = jnp.clip(r - kr // 2, 0, rows - kr)
    key_rows = rs[:, None] + jnp.arange(kr)
    cb = jnp.arange(ncb)
    kc0 = jnp.clip(cb * NA_COLS - NA_COLS // 2, 0, GRID_W - kcw)
    key_cols = kc0[:, None] + jnp.arange(kcw)
    kidx = (key_rows[:, None, :, None] * GRID_W + key_cols[None, :, None, :]).reshape(rows, ncb, nk)
    krow_b = jnp.broadcast_to(key_rows[:, None, :, None], (rows, ncb, kr, kcw)).reshape(rows, ncb, nk)
    kcol_b = jnp.broadcast_to(key_cols[None, :, None, :], (rows, ncb, kr, kcw)).reshape(rows, ncb, nk)
    q_col = cb[:, None] * NA_COLS + jnp.arange(NA_COLS)
    cs = jnp.clip(q_col - NA_COLS // 2, 0, GRID_W - NA_COLS)
    kcol_q = kcol_b[:, :, None, :]
    valid = (kcol_q >= cs[None, :, :, None]) & (kcol_q < cs[None, :, :, None] + NA_COLS)
    dc = kcol_q - q_col[None, :, :, None]
    dr = jnp.broadcast_to(krow_b[:, :, None, :] - r[:, None, None, None], dc.shape)
    bias = rpb[:, dr + NA_ROWS - 1, jnp.clip(dc + NA_COLS - 1, 0, 2 * NA_COLS - 2)]
    flat = kidx.reshape(-1)
    kg = k[:, flat].reshape(B, rows, ncb, nk, H, d)
    vg = v[:, flat].reshape(B, rows, ncb, nk, H, d)
    qb = q.reshape(B, rows, ncb, NA_COLS, H, d)
    scale = d ** -0.5
    s_nb = jnp.einsum("brcqhd,brckhd->brchqk", qb, kg, preferred_element_type=jnp.float32) * scale
    s_nb = s_nb + bias.transpose(1, 2, 0, 3, 4)[None].astype(jnp.float32)
    s_nb = jnp.where(valid[None, :, :, None], s_nb, NEG_INF)
    s_ctx = jnp.einsum("brcqhd,bkhd->brchqk", qb, kc, preferred_element_type=jnp.float32) * scale
    p = jax.nn.softmax(jnp.concatenate([s_nb, s_ctx], axis=-1), axis=-1).astype(v.dtype)
    o = (jnp.einsum("brchqk,brckhd->brcqhd", p[..., :nk], vg)
         + jnp.einsum("brchqk,bkhd->brcqhd", p[..., nk:], vc))
    return o.reshape(B, S, H, d)


def window_attention(q, k, v, kc, vc, sink):
    B, S, H, d = q.shape
    G = k.shape[2]
    rep = H // G
    W = SWA_WINDOW
    nb = S // W
    kpad = jnp.pad(k, ((0, 0), (W, W), (0, 0), (0, 0)))
    vpad = jnp.pad(v, ((0, 0), (W, W), (0, 0), (0, 0)))
    kidx = jnp.arange(nb)[:, None] * W + jnp.arange(3 * W)[None, :]
    kb = kpad[:, kidx.reshape(-1)].reshape(B, nb, 3 * W, G, d)
    vb = vpad[:, kidx.reshape(-1)].reshape(B, nb, 3 * W, G, d)
    qpos = jnp.arange(nb)[:, None] * W + jnp.arange(W)[None, :]
    kpos = kidx - W
    valid = ((jnp.abs(kpos[:, None, :] - qpos[:, :, None]) <= W)
             & (kpos[:, None, :] >= 0) & (kpos[:, None, :] < S))
    qb = q.reshape(B, nb, W, G, rep, d)
    scale = d ** -0.5
    s_w = jnp.einsum("bnqgrd,bnkgd->bngrqk", qb, kb, preferred_element_type=jnp.float32) * scale
    s_w = jnp.where(valid[None, :, None, None], s_w, NEG_INF)
    s_c = jnp.einsum("bnqgrd,bkgd->bngrqk", qb, kc, preferred_element_type=jnp.float32) * scale
    s_sink = jnp.broadcast_to(sink.astype(jnp.float32).reshape(1, 1, G, rep, 1, 1), s_w.shape[:-1] + (1,))
    p = jax.nn.softmax(jnp.concatenate([s_w, s_c, s_sink], axis=-1), axis=-1).astype(v.dtype)
    nw = 3 * W
    P = kc.shape[1]
    o = (jnp.einsum("bngrqk,bnkgd->bnqgrd", p[..., :nw], vb)
         + jnp.einsum("bngrqk,bkgd->bnqgrd", p[..., nw:nw + P], vc))
    return o.reshape(B, S, H, d)


def mla_mixer(q_lat, ckv, krope, prm, pos, cache):
    B, S, _ = q_lat.shape
    q = (rms_norm(q_lat, prm["mla_q_norm_g"]) @ prm["mla_w_uq"]).reshape(B, S, MLA_HEADS, MLA_NOPE + MLA_ROPE)
    q_nope, q_rope = q[..., :MLA_NOPE], q[..., MLA_NOPE:]
    ckv = rms_norm(ckv, prm["mla_kv_norm_g"])
    if pos is None:
        ckv_all, kr_all = ckv, krope
    else:
        rows, cols = pos
        q_rope = axial_rope(q_rope, rows, cols)
        kr_lat = axial_rope(krope[:, :, None, :], rows, cols)[:, :, 0, :]
        ckv_all = jnp.concatenate([cache[0], ckv], axis=1)
        kr_all = jnp.concatenate([cache[1], kr_lat], axis=1)
    T = ckv_all.shape[1]
    k_nope = (ckv_all @ prm["mla_w_uk"]).reshape(B, T, MLA_HEADS, MLA_NOPE)
    v = (ckv_all @ prm["mla_w_uv"]).reshape(B, T, MLA_HEADS, MLA_V)
    k = jnp.concatenate([k_nope, jnp.broadcast_to(kr_all[:, :, None, :], (B, T, MLA_HEADS, MLA_ROPE))], axis=-1)
    o = dense_attention(jnp.concatenate([q_nope, q_rope], axis=-1), k, v)
    return o.reshape(B, S, MLA_HEADS * MLA_V), (ckv, krope)


def na_mixer(q, k, v, prm, pos, cache):
    B, S, _ = q.shape
    q = q.reshape(B, S, NA_HEADS, HEAD_DIM)
    k = k.reshape(B, S, NA_HEADS, HEAD_DIM)
    v = v.reshape(B, S, NA_HEADS, HEAD_DIM)
    if pos is None:
        o = dense_attention(q, k, v)
    else:
        o = neighborhood_attention(q, k, v, cache[0], cache[1], prm["na_rpb"])
    return o.reshape(B, S, NA_HEADS * HEAD_DIM), (k, v)


def gqa_mixer(q, k, v, prm, pos, cache):
    B, S, _ = q.shape
    q = rms_norm(q.reshape(B, S, GQA_HEADS, HEAD_DIM), prm["gqa_q_norm_g"])
    k = rms_norm(k.reshape(B, S, GQA_KV_HEADS, HEAD_DIM), prm["gqa_k_norm_g"])
    v = v.reshape(B, S, GQA_KV_HEADS, HEAD_DIM)
    if pos is None:
        o = dense_attention(q, k, v)
    else:
        rows, cols = pos
        k_all = jnp.concatenate([cache[0], axial_rope(k, rows, cols)], axis=1)
        v_all = jnp.concatenate([cache[1], v], axis=1)
        o = dense_attention(axial_rope(q, rows, cols), k_all, v_all)
    return o.reshape(B, S, GQA_HEADS * HEAD_DIM), (k, v)


def swa_mixer(q, k, v, prm, pos, cache):
    B, S, _ = q.shape
    q = q.reshape(B, S, SWA_HEADS, HEAD_DIM)
    k = k.reshape(B, S, SWA_KV_HEADS, HEAD_DIM)
    v = v.reshape(B, S, SWA_KV_HEADS, HEAD_DIM)
    if pos is None:
        o = dense_attention(q, k, v, sink=prm["swa_sink"])
    else:
        rows, cols = pos
        o = window_attention(axial_rope(q, rows, cols), axial_rope(k, rows, cols), v,
                             cache[0], cache[1], prm["swa_sink"])
    return o.reshape(B, S, SWA_HEADS * HEAD_DIM), (k, v)


def peer_ffn(h, prm):
    B, S, D = h.shape
    x = h.reshape(-1, D)
    T = x.shape[0]
    q = (x @ prm["peer_w_q"]).reshape(T, PEER_HEADS, 2, PEER_D_KEY // 2)
    s = jnp.einsum("thpk,hpnk->thpn", q, prm["peer_subkeys"], preferred_element_type=jnp.float32)
    s_top, i_top = lax.top_k(s, PEER_TOPK)
    cand_s = s_top[:, :, 0, :, None] + s_top[:, :, 1, None, :]
    cand_i = i_top[:, :, 0, :, None] * PEER_N_KEYS + i_top[:, :, 1, None, :]
    best_s, best_j = lax.top_k(cand_s.reshape(T, PEER_HEADS, PEER_TOPK * PEER_TOPK), PEER_TOPK)
    expert = jnp.take_along_axis(cand_i.reshape(T, PEER_HEADS, PEER_TOPK * PEER_TOPK), best_j, axis=-1)
    gate = jax.nn.softmax(best_s, axis=-1)
    nb = T // PEER_TOKEN_BLOCK
    ne = PEER_HEADS * PEER_TOPK
    u_tab, v_tab = prm["peer_u"], prm["peer_v"]

    def one_block(args):
        xb, eb, gb = args
        act = jax.nn.gelu(jnp.einsum("td,ted->te", xb, u_tab[eb]), approximate=False)
        w = (gb * act.astype(jnp.float32)).astype(v_tab.dtype)
        return jnp.einsum("te,ted->td", w, v_tab[eb])

    out = lax.map(one_block, (x.reshape(nb, PEER_TOKEN_BLOCK, D),
                              expert.reshape(nb, PEER_TOKEN_BLOCK, ne),
                              gate.reshape(nb, PEER_TOKEN_BLOCK, ne)))
    return out.reshape(B, S, D)


def trunk_layer(x, cond, prm, pos, cache):
    shift1, scale1, gate1, shift2, scale2, gate2 = modulation(cond, prm["w_mod"], prm["b_mod"])
    h = rms_norm(x, prm["norm1_g"]) * (1.0 + scale1) + shift1
    points = np.cumsum(np.array(IN_SPLITS))[:-1].tolist()
    (q_lat, ckv, krope, na_q, na_k, na_v, c_q, c_k, c_v, d_q, d_k, d_v) = jnp.split(h @ prm["w_in"], points, axis=-1)
    ca = None if cache is None else cache[0:2]
    cb = None if cache is None else cache[2:4]
    cc = None if cache is None else cache[4:6]
    cd = None if cache is None else cache[6:8]
    o_a, st_a = mla_mixer(q_lat, ckv, krope, prm, pos, ca)
    o_b, st_b = na_mixer(na_q, na_k, na_v, prm, pos, cb)
    o_c, st_c = gqa_mixer(c_q, c_k, c_v, prm, pos, cc)
    o_d, st_d = swa_mixer(d_q, d_k, d_v, prm, pos, cd)
    mix = jnp.concatenate([o_a, o_b, o_c, o_d], axis=-1)
    x = x + gate1 * (mix @ prm["w_out"])
    h2 = rms_norm(x, prm["norm2_g"]) * (1.0 + scale2) + shift2
    x = x + gate2 * peer_ffn(h2, prm)
    return x, st_a + st_b + st_c + st_d


def setup_inputs(seed: int = 0) -> dict:
    key = jax.random.key(seed)
    ks = jax.random.split(key, 40)
    f32 = jnp.float32

    def nrm(k, shape, scale):
        return jax.random.normal(k, shape, f32) * scale

    def gain(k, shape):
        return 1.0 + 0.01 * jax.random.normal(k, shape, f32)

    return {
        "x_prompt": nrm(ks[0], (BATCH, SEQ, D_MODEL), 1.0),
        "x_sample": nrm(ks[1], (DEC_BATCH, DEC_SEQ, D_MODEL), 1.0),
        "c": nrm(ks[2], (DEC_BATCH, D_MODEL), 1.0),
        "cache_mla_ckv": nrm(ks[3], (DEC_BATCH, DEPTH, PAST_LEN, MLA_KV_RANK), 1.0),
        "cache_mla_krope": nrm(ks[4], (DEC_BATCH, DEPTH, PAST_LEN, MLA_ROPE), 1.0),
        "cache_na_k": nrm(ks[5], (DEC_BATCH, DEPTH, PAST_LEN, NA_HEADS, HEAD_DIM), 1.0),
        "cache_na_v": nrm(ks[6], (DEC_BATCH, DEPTH, PAST_LEN, NA_HEADS, HEAD_DIM), 1.0),
        "cache_gqa_k": nrm(ks[7], (DEC_BATCH, DEPTH, PAST_LEN, GQA_KV_HEADS, HEAD_DIM), 1.0),
        "cache_gqa_v": nrm(ks[8], (DEC_BATCH, DEPTH, PAST_LEN, GQA_KV_HEADS, HEAD_DIM), 1.0),
        "cache_swa_k": nrm(ks[9], (DEC_BATCH, DEPTH, PAST_LEN, SWA_KV_HEADS, HEAD_DIM), 1.0),
        "cache_swa_v": nrm(ks[10], (DEC_BATCH, DEPTH, PAST_LEN, SWA_KV_HEADS, HEAD_DIM), 1.0),
        "c_ctx": nrm(ks[11], (D_MODEL,), 1.0),
        "w_mod": nrm(ks[12], (DEPTH, D_MODEL, N_MOD * D_MODEL), D_MODEL ** -0.5),
        "b_mod": nrm(ks[13], (DEPTH, N_MOD * D_MODEL), 0.01),
        "norm1_g": gain(ks[14], (DEPTH, D_MODEL)),
        "w_in": nrm(ks[15], (DEPTH, D_MODEL, D_IN), D_MODEL ** -0.5),
        "mla_q_norm_g": gain(ks[16], (DEPTH, MLA_Q_RANK)),
        "mla_w_uq": nrm(ks[17], (DEPTH, MLA_Q_RANK, MLA_HEADS * (MLA_NOPE + MLA_ROPE)), MLA_Q_RANK ** -0.5),
        "mla_kv_norm_g": gain(ks[18], (DEPTH, MLA_KV_RANK)),
        "mla_w_uk": nrm(ks[19], (DEPTH, MLA_KV_RANK, MLA_HEADS * MLA_NOPE), MLA_KV_RANK ** -0.5),
        "mla_w_uv": nrm(ks[20], (DEPTH, MLA_KV_RANK, MLA_HEADS * MLA_V), MLA_KV_RANK ** -0.5),
        "na_rpb": nrm(ks[21], (DEPTH, NA_HEADS, 2 * NA_ROWS - 1, 2 * NA_COLS - 1), 0.1),
        "gqa_q_norm_g": gain(ks[22], (DEPTH, HEAD_DIM)),
        "gqa_k_norm_g": gain(ks[23], (DEPTH, HEAD_DIM)),
        "swa_sink": nrm(ks[24], (DEPTH, SWA_HEADS), 0.5),
        "w_out": nrm(ks[25], (DEPTH, D_MIX, D_MODEL), D_MIX ** -0.5),
        "norm2_g": gain(ks[26], (DEPTH, D_MODEL)),
        "peer_w_q": nrm(ks[27], (DEPTH, D_MODEL, PEER_HEADS * PEER_D_KEY), D_MODEL ** -0.5),
        "peer_subkeys": nrm(ks[28], (DEPTH, PEER_HEADS, 2, PEER_N_KEYS, PEER_D_KEY // 2), (PEER_D_KEY // 2) ** -0.5),
        "peer_u": nrm(ks[29], (DEPTH, PEER_N_EXPERTS, D_MODEL), D_MODEL ** -0.5),
        "peer_v": nrm(ks[30], (DEPTH, PEER_N_EXPERTS, D_MODEL), 0.5),
        "final_norm_g": gain(ks[31], (D_MODEL,)),
    }


def reference(x_prompt, x_sample, c, cache_mla_ckv, cache_mla_krope, cache_na_k, cache_na_v,
              cache_gqa_k, cache_gqa_v, cache_swa_k, cache_swa_v, c_ctx, w_mod, b_mod, norm1_g, w_in,
              mla_q_norm_g, mla_w_uq, mla_kv_norm_g, mla_w_uk, mla_w_uv, na_rpb, gqa_q_norm_g,
              gqa_k_norm_g, swa_sink, w_out, norm2_g, peer_w_q, peer_subkeys, peer_u, peer_v,
              final_norm_g):
    pos = grid_positions(x_sample.shape[1])
    xp, xs = x_prompt, x_sample
    states = []
    for l in range(DEPTH):
        prm = {
            "w_mod": w_mod[l], "b_mod": b_mod[l], "norm1_g": norm1_g[l], "w_in": w_in[l],
            "mla_q_norm_g": mla_q_norm_g[l], "mla_w_uq": mla_w_uq[l], "mla_kv_norm_g": mla_kv_norm_g[l],
            "mla_w_uk": mla_w_uk[l], "mla_w_uv": mla_w_uv[l], "na_rpb": na_rpb[l],
            "gqa_q_norm_g": gqa_q_norm_g[l], "gqa_k_norm_g": gqa_k_norm_g[l], "swa_sink": swa_sink[l],
            "w_out": w_out[l], "norm2_g": norm2_g[l], "peer_w_q": peer_w_q[l],
            "peer_subkeys": peer_subkeys[l], "peer_u": peer_u[l], "peer_v": peer_v[l],
        }
        xp, st = trunk_layer(xp, c_ctx[None, :], prm, None, None)
        states.append(st)
        cache_l = (cache_mla_ckv[:, l], cache_mla_krope[:, l], cache_na_k[:, l], cache_na_v[:, l],
                   cache_gqa_k[:, l], cache_gqa_v[:, l], cache_swa_k[:, l], cache_swa_v[:, l])
        xs, _ = trunk_layer(xs, c, prm, pos, cache_l)
    y_prompt = rms_norm(xp, final_norm_g)
    y_sample = rms_norm(xs, final_norm_g)
    new_mla_ckv = jnp.stack([s[0] for s in states], axis=1)
    new_mla_krope = jnp.stack([s[1] for s in states], axis=1)
    new_na_k = jnp.stack([s[2] for s in states], axis=1)
    new_na_v = jnp.stack([s[3] for s in states], axis=1)
    new_gqa_k = jnp.stack([s[4] for s in states], axis=1)
    new_gqa_v = jnp.stack([s[5] for s in states], axis=1)
    new_swa_k = jnp.stack([s[6] for s in states], axis=1)
    new_swa_v = jnp.stack([s[7] for s in states], axis=1)
    return (y_prompt, y_sample, new_mla_ckv, new_mla_krope, new_na_k, new_na_v,
            new_gqa_k, new_gqa_v, new_swa_k, new_swa_v)
```

```python
import functools

import numpy as np
import jax
import jax.numpy as jnp
from jax import lax
from jax.experimental import pallas as pl
from jax.experimental.pallas import tpu as pltpu

F32 = jnp.float32
BF16 = jnp.bfloat16

D = 2048
N_CTX_B, CTX_S = 16, 256
N_LAT_B, LAT_S = 4, 1024
N_CTX = N_CTX_B * CTX_S
N_TOK = N_CTX + N_LAT_B * LAT_S
DEPTH = 2
PAST = 512
GRID_W = 64
HD = 128
MLA_Q_RANK, MLA_KV_RANK, MLA_ROPE = 384, 256, 64
NA_ROWS, NA_COLS = 8, 16
SWA_W = 128
N_KEYS = 128
PEER_HEADS = 8
PEER_TOPK = 16
N_MOD = 6
EPS = 1e-6
NEG_INF = -1e30
ROPE_THETA = 10000.0
LANE = 128

IN_SPLITS = (384, 256, 64, 512, 512, 512, 512, 256, 256, 512, 256, 256)
D_IN_P = 4608
COL = dict(na_q=0, na_k=4, na_v=8, c_q=12, d_q=16, q_lat=20, ckv=24, c_k=26, c_v=28, d_k=30, d_v=32,
           krope=34)

VMEM_LIMIT = 56 * 1024 * 1024


def _params(sem):
    return pltpu.CompilerParams(dimension_semantics=sem, vmem_limit_bytes=VMEM_LIMIT)


def _full(shape):
    nd = len(shape)
    return pl.BlockSpec(shape, lambda *_: (0,) * nd)


def _dot(a, b):
    return jnp.dot(a, b, preferred_element_type=F32)


def _dot_nt(a, b):
    return lax.dot_general(a, b, (((1,), (1,)), ((), ())), preferred_element_type=F32)


def _cond_of_rows(row0):
    return jnp.where(row0 < N_CTX, 0, 1 + (row0 - N_CTX) // LAT_S)


def _mod_kernel(c_ref, w_ref, b_ref, o_ref):
    c = c_ref[...]
    s = c * jax.nn.sigmoid(c)
    o_ref[0] = _dot(s.astype(BF16), w_ref[0].astype(BF16)) + b_ref[0]


def _modulation(cond8, w_mod, b_mod):
    tn = 1536
    n = N_MOD * D
    return pl.pallas_call(
        _mod_kernel,
        grid=(DEPTH, n // tn),
        in_specs=[pl.BlockSpec((8, D), lambda l, j: (0, 0)),
                  pl.BlockSpec((1, D, tn), lambda l, j: (l, 0, j)),
                  pl.BlockSpec((1, 1, tn), lambda l, j: (l, 0, j))],
        out_specs=pl.BlockSpec((1, 8, tn), lambda l, j: (l, 0, j)),
        out_shape=jax.ShapeDtypeStruct((DEPTH, 8, n), F32),
        compiler_params=_params(("arbitrary", "arbitrary")),
        name="modulation",
    )(cond8, w_mod, b_mod.reshape(DEPTH, 1, n))


def _inproj_kernel(x_ref, mod_ref, g_ref, w_ref, o_ref):
    x = x_ref[...]
    y = x * lax.rsqrt(jnp.mean(x * x, axis=-1, keepdims=True) + EPS) * g_ref[...]
    m = mod_ref[0]
    h = y * (1.0 + m[1:2, :]) + m[0:1, :]
    o_ref[...] = _dot(h.astype(BF16), w_ref[...])


def _inproj(x, mod, g, w_p):
    tm, tn = 512, 2304
    return pl.pallas_call(
        _inproj_kernel,
        grid=(D_IN_P // tn, N_TOK // tm),
        in_specs=[pl.BlockSpec((tm, D), lambda j, i: (i, 0)),
                  pl.BlockSpec((1, N_MOD, D), lambda j, i: (_cond_of_rows(i * tm), 0, 0)),
                  pl.BlockSpec((1, D), lambda j, i: (0, 0)),
                  pl.BlockSpec((D, tn), lambda j, i: (0, j))],
        out_specs=pl.BlockSpec((tm, tn), lambda j, i: (i, j)),
        out_shape=jax.ShapeDtypeStruct((N_TOK, D_IN_P), F32),
        compiler_params=_params(("arbitrary", "arbitrary")),
        name="inproj",
    )(x, mod, g, w_p)


def _rms(x, g, n=None):
    n = x.shape[-1] if n is None else n
    return x * lax.rsqrt(jnp.sum(x * x, axis=-1, keepdims=True) * (1.0 / n) + EPS) * g


def _rope(x, cos, sin_lo, sin_hi, shift):
    return (x * cos + pltpu.roll(x, LANE - shift, 1) * sin_lo + pltpu.roll(x, shift, 1) * sin_hi)


def _softmax_pv(parts, sink=None):
    m = parts[0][0].max(axis=-1, keepdims=True)
    for s, _ in parts[1:]:
        m = jnp.maximum(m, s.max(axis=-1, keepdims=True))
    if sink is not None:
        m = jnp.maximum(m, sink)
    es = [jnp.exp(s - m) for s, _ in parts]
    den = es[0].sum(axis=-1, keepdims=True)
    for e in es[1:]:
        den = den + e.sum(axis=-1, keepdims=True)
    if sink is not None:
        den = den + jnp.exp(sink - m)
    inv = 1.0 / den
    o = None
    for e, (_, v) in zip(es, parts):
        t = _dot((e * inv).astype(BF16), v)
        o = t if o is None else o + t
    return o


QB = 256


def _mla_kernel(*refs, latent, s_len):
    if latent:
        (ql_ref, ckv_ref, kr_ref, cckv_ref, ckr_ref, cos_ref, sa_ref, sb_ref,
         qg_ref, wuq_ref, kvg_ref, wuk_ref, wuv_ref, o_ref, ckv_scr, q_scr, k_scr, v_scr) = refs
    else:
        (ql_ref, ckv_ref, kr_ref, qg_ref, wuq_ref, kvg_ref, wuk_ref, wuv_ref,
         o_ref, ckv_out, kr_out, ckv_scr, q_scr, k_scr, v_scr) = refs
    p_len = PAST if latent else 0
    qn = _rms(ql_ref[...], qg_ref[...], MLA_Q_RANK).astype(BF16)
    ckv_n = _rms(ckv_ref[...], kvg_ref[...])
    kr = kr_ref[...]
    if latent:
        cos, sa, sb = cos_ref[...], sa_ref[...], sb_ref[...]
        kr = _rope(kr, cos, sa, sb, 32)
        ckv_scr[0:p_len, :] = cckv_ref[0, 0].astype(BF16)
        kr_all = jnp.concatenate([ckr_ref[0, 0], kr], axis=0)
    else:
        ckv_out[...] = ckv_n
        kr_out[...] = kr[:, :MLA_ROPE]
        kr_all = kr
    ckv_scr[p_len:, :] = ckv_n.astype(BF16)
    kr_all = kr_all.astype(BF16)
    ckv_all = ckv_scr[...]
    for h in range(4):
        q_scr[h, :, 0:HD] = _dot(qn, wuq_ref[:, h * HD:(h + 1) * HD]).astype(BF16)
        qr = _dot(qn, wuq_ref[:, (4 + h) * HD:(5 + h) * HD])
        if latent:
            qr = _rope(qr, cos, sa, sb, 32)
        q_scr[h, :, HD:2 * HD] = qr.astype(BF16)
        k_scr[h, :, 0:HD] = _dot(ckv_all, wuk_ref[:, h * HD:(h + 1) * HD]).astype(BF16)
        k_scr[h, :, HD:2 * HD] = kr_all
        v_scr[h] = _dot(ckv_all, wuv_ref[:, h * HD:(h + 1) * HD]).astype(BF16)
    scale = float(HD + MLA_ROPE) ** -0.5
    for h in range(4):
        for qb in range(s_len // QB):
            s = _dot_nt(q_scr[h, qb * QB:(qb + 1) * QB, :], k_scr[h]) * scale
            o = _softmax_pv([(s, v_scr[h])])
            o_ref[qb * QB:(qb + 1) * QB, h * HD:(h + 1) * HD] = o.astype(BF16)


def _mla(proj, wts, l, latent, caches=None, tabs=None):
    s_len = LAT_S if latent else CTX_S
    nb = N_LAT_B if latent else N_CTX_B
    rb = (N_CTX // s_len) if latent else 0
    t_len = s_len + (PAST if latent else 0)
    qg, wuq, kvg, wuk, wuv = wts
    in_specs = [pl.BlockSpec((s_len, 512), lambda b: (rb + b, COL["q_lat"] // 4)),
                pl.BlockSpec((s_len, 256), lambda b: (rb + b, COL["ckv"] // 2)),
                pl.BlockSpec((s_len, 128), lambda b: (rb + b, COL["krope"]))]
    args = [proj, proj, proj]
    if latent:
        in_specs += [pl.BlockSpec((1, 1, PAST, 256), lambda b: (b, l, 0, 0)),
                     pl.BlockSpec((1, 1, PAST, 128), lambda b: (b, l, 0, 0)),
                     _full((s_len, 128)), _full((s_len, 128)), _full((s_len, 128))]
        args += [caches[0], caches[1], *tabs]
    in_specs += [_full((1, 512)), _full((512, 1024)), _full((1, 256)), _full((256, 512)), _full((256, 512))]
    args += [qg, wuq, kvg, wuk, wuv]
    out_specs = [pl.BlockSpec((s_len, 512), lambda b: (b, 0))]
    out_shape = [jax.ShapeDtypeStruct((nb * s_len, 512), BF16)]
    if not latent:
        out_specs += [pl.BlockSpec((s_len, 256), lambda b: (b, 0)), pl.BlockSpec((s_len, 64), lambda b: (b, 0))]
        out_shape += [jax.ShapeDtypeStruct((nb * s_len, 256), F32), jax.ShapeDtypeStruct((nb * s_len, 64), F32)]
    return pl.pallas_call(
        functools.partial(_mla_kernel, latent=latent, s_len=s_len),
        grid=(nb,), in_specs=in_specs, out_specs=out_specs, out_shape=out_shape,
        scratch_shapes=[pltpu.VMEM((t_len, 256), BF16), pltpu.VMEM((4, s_len, 256), BF16),
                        pltpu.VMEM((4, t_len, 256), BF16), pltpu.VMEM((4, t_len, HD), BF16)],
        compiler_params=_params(("arbitrary",)),
        name="mla_latent" if latent else "mla_context",
    )(*args)


def _na_kernel(*refs, latent, s_len):
    if latent:
        q_ref, k_ref, v_ref, ck_ref, cv_ref, bias_ref, o_ref = refs
    else:
        q_ref, k_ref, v_ref, o_ref, k_out, v_out = refs
        k_out[...] = k_ref[...]
        v_out[...] = v_ref[...]
    scale = float(HD) ** -0.5
    rows = s_len // GRID_W
    for h in range(4):
        cs = slice(h * HD, (h + 1) * HD)
        if not latent:
            s = _dot_nt(q_ref[:, cs].astype(BF16), k_ref[:, cs].astype(BF16)) * scale
            o_ref[:, cs] = _softmax_pv([(s, v_ref[:, cs].astype(BF16))]).astype(BF16)
            continue
        kc = ck_ref[0, 0, :, cs].astype(BF16)
        vc = cv_ref[0, 0, :, cs].astype(BF16)
        for r in range(rows):
            r0 = min(max(r - NA_ROWS // 2, 0), rows - NA_ROWS) * GRID_W
            win = slice(r0, r0 + NA_ROWS * GRID_W)
            q = q_ref[r * GRID_W:(r + 1) * GRID_W, cs].astype(BF16)
            s_nb = _dot_nt(q, k_ref[win, cs].astype(BF16)) * scale + bias_ref[h, r]
            s_c = _dot_nt(q, kc) * scale
            o = _softmax_pv([(s_nb, v_ref[win, cs].astype(BF16)), (s_c, vc)])
            o_ref[r * GRID_W:(r + 1) * GRID_W, cs] = o.astype(BF16)


def _na(proj, l, latent, caches=None, bias=None):
    s_len = LAT_S if latent else CTX_S
    nb = N_LAT_B if latent else N_CTX_B
    rb = (N_CTX // s_len) if latent else 0
    in_specs = [pl.BlockSpec((s_len, 512), lambda b: (rb + b, COL["na_q"] // 4)),
                pl.BlockSpec((s_len, 512), lambda b: (rb + b, COL["na_k"] // 4)),
                pl.BlockSpec((s_len, 512), lambda b: (rb + b, COL["na_v"] // 4))]
    args = [proj, proj, proj]
    out_specs = [pl.BlockSpec((s_len, 512), lambda b: (b, 0))]
    out_shape = [jax.ShapeDtypeStruct((nb * s_len, 512), BF16)]
    if latent:
        in_specs += [pl.BlockSpec((1, 1, PAST, 512), lambda b: (b, l, 0, 0)),
                     pl.BlockSpec((1, 1, PAST, 512), lambda b: (b, l, 0, 0)),
                     _full(bias.shape)]
        args += [caches[0], caches[1], bias]
    else:
        out_specs += [pl.BlockSpec((s_len, 512), lambda b: (b, 0))] * 2
        out_shape += [jax.ShapeDtypeStruct((nb * s_len, 512), F32)] * 2
    return pl.pallas_call(
        functools.partial(_na_kernel, latent=latent, s_len=s_len),
        grid=(nb,), in_specs=in_specs, out_specs=out_specs, out_shape=out_shape,
        compiler_params=_params(("arbitrary",)),
        name="na_latent" if latent else "na_context",
    )(*args)


def _na_bias(rpb):
    rows = LAT_S // GRID_W
    r = np.arange(rows)
    rs = np.clip(r - NA_ROWS // 2, 0, rows - NA_ROWS)
    j = np.arange(NA_ROWS)
    dr = (rs[:, None] + j[None, :]) - r[:, None] + NA_ROWS - 1
    qc = np.arange(GRID_W)
    kc = np.arange(GRID_W)
    dc = kc[None, :] - qc[:, None] + NA_COLS - 1
    cs = np.clip(qc - NA_COLS // 2, 0, GRID_W - NA_COLS)
    valid = (kc[None, :] >= cs[:, None]) & (kc[None, :] < cs[:, None] + NA_COLS)
    dcc = np.clip(dc, 0, 2 * NA_COLS - 2)
    dr_f = np.broadcast_to(dr[:, None, :, None], (rows, GRID_W, NA_ROWS, GRID_W)).reshape(rows, GRID_W, -1)
    dc_f = np.broadcast_to(dcc[None, :, None, :], (rows, GRID_W, NA_ROWS, GRID_W)).reshape(rows, GRID_W, -1)
    va_f = np.broadcast_to(valid[None, :, None, :], (rows, GRID_W, NA_ROWS, GRID_W)).reshape(rows, GRID_W, -1)
    b = rpb[:, dr_f, dc_f]
    return jnp.where(va_f[None], b, NEG_INF).astype(F32)


def _gqa_kernel(*refs, latent, s_len):
    if latent:
        q_ref, k_ref, v_ref, ck_ref, cv_ref, cos_ref, sl_ref, sh_ref, qg_ref, kg_ref, o_ref, k_scr, v_scr = refs
        cos, sl, sh = cos_ref[...], sl_ref[...], sh_ref[...]
    else:
        q_ref, k_ref, v_ref, qg_ref, kg_ref, o_ref, k_out, v_out, k_scr, v_scr = refs
        v_out[...] = v_ref[...]
    p_len = PAST if latent else 0
    scale = float(HD) ** -0.5
    for g in range(2):
        cs = slice(g * HD, (g + 1) * HD)
        kn = _rms(k_ref[:, cs], kg_ref[...])
        if latent:
            kn = _rope(kn, cos, sl, sh, 64)
            k_scr[g, 0:p_len, :] = ck_ref[0, 0, :, cs].astype(BF16)
            v_scr[g, 0:p_len, :] = cv_ref[0, 0, :, cs].astype(BF16)
        else:
            k_out[:, cs] = kn
        k_scr[g, p_len:, :] = kn.astype(BF16)
        v_scr[g, p_len:, :] = v_ref[:, cs].astype(BF16)
    for hq in range(4):
        g = hq // 2
        cs = slice(hq * HD, (hq + 1) * HD)
        for qb in range(s_len // QB):
            rs = slice(qb * QB, (qb + 1) * QB)
            q = _rms(q_ref[rs, cs], qg_ref[...])
            if latent:
                q = _rope(q, cos[rs], sl[rs], sh[rs], 64)
            s = _dot_nt(q.astype(BF16), k_scr[g]) * scale
            o_ref[rs, cs] = _softmax_pv([(s, v_scr[g])]).astype(BF16)


def _gqa(proj, wts, l, latent, caches=None, tabs=None):
    s_len = LAT_S if latent else CTX_S
    nb = N_LAT_B if latent else N_CTX_B
    rb = (N_CTX // s_len) if latent else 0
    t_len = s_len + (PAST if latent else 0)
    in_specs = [pl.BlockSpec((s_len, 512), lambda b: (rb + b, COL["c_q"] // 4)),
                pl.BlockSpec((s_len, 256), lambda b: (rb + b, COL["c_k"] // 2)),
                pl.BlockSpec((s_len, 256), lambda b: (rb + b, COL["c_v"] // 2))]
    args = [proj, proj, proj]
    out_specs = [pl.BlockSpec((s_len, 512), lambda b: (b, 0))]
    out_shape = [jax.ShapeDtypeStruct((nb * s_len, 512), BF16)]
    if latent:
        in_specs += [pl.BlockSpec((1, 1, PAST, 256), lambda b: (b, l, 0, 0)),
                     pl.BlockSpec((1, 1, PAST, 256), lambda b: (b, l, 0, 0)),
                     _full((s_len, 128)), _full((s_len, 128)), _full((s_len, 128))]
        args += [caches[0], caches[1], *tabs]
    else:
        out_specs += [pl.BlockSpec((s_len, 256), lambda b: (b, 0))] * 2
        out_shape += [jax.ShapeDtypeStruct((nb * s_len, 256), F32)] * 2
    in_specs += [_full((1, HD)), _full((1, HD))]
    args += list(wts)
    return pl.pallas_call(
        functools.partial(_gqa_kernel, latent=latent, s_len=s_len),
        grid=(nb,), in_specs=in_specs, out_specs=out_specs, out_shape=out_shape,
        scratch_shapes=[pltpu.VMEM((2, t_len, HD), BF16), pltpu.VMEM((2, t_len, HD), BF16)],
        compiler_params=_params(("arbitrary",)),
        name="gqa_latent" if latent else "gqa_context",
    )(*args)


def _swa_kernel(*refs, latent, s_len):
    if latent:
        q_ref, k_ref, v_ref, ck_ref, cv_ref, cos_ref, sl_ref, sh_ref, sink_ref, o_ref, k_scr, v_scr = refs
        cos, sl, sh = cos_ref[...], sl_ref[...], sh_ref[...]
    else:
        q_ref, k_ref, v_ref, sink_ref, o_ref, k_out, v_out, k_scr, v_scr = refs
        k_out[...] = k_ref[...]
        v_out[...] = v_ref[...]
    scale = float(HD) ** -0.5
    for g in range(2):
        cs = slice(g * HD, (g + 1) * HD)
        k = k_ref[:, cs]
        if latent:
            k = _rope(k, cos, sl, sh, 64)
        k_scr[g] = k.astype(BF16)
        v_scr[g] = v_ref[:, cs].astype(BF16)
    for hq in range(4):
        g = hq // 2
        cs = slice(hq * HD, (hq + 1) * HD)
        gs = slice(g * HD, (g + 1) * HD)
        sink = sink_ref[hq]
        if not latent:
            s = _dot_nt(q_ref[:, cs].astype(BF16), k_scr[g]) * scale
            o_ref[:, cs] = _softmax_pv([(s, v_scr[g])], sink=sink).astype(BF16)
            continue
        kc = ck_ref[0, 0, :, gs].astype(BF16)
        vc = cv_ref[0, 0, :, gs].astype(BF16)
        for n in range(s_len // SWA_W):
            rs = slice(n * SWA_W, (n + 1) * SWA_W)
            lo, hi = max(0, (n - 1) * SWA_W), min(s_len, (n + 2) * SWA_W)
            q = _rope(q_ref[rs, cs], cos[rs], sl[rs], sh[rs], 64).astype(BF16)
            s_w = _dot_nt(q, k_scr[g, lo:hi, :]) * scale
            qpos = n * SWA_W + lax.broadcasted_iota(jnp.int32, s_w.shape, 0)
            kpos = lo + lax.broadcasted_iota(jnp.int32, s_w.shape, 1)
            s_w = jnp.where(jnp.abs(kpos - qpos) <= SWA_W, s_w, NEG_INF)
            s_c = _dot_nt(q, kc) * scale
            o = _softmax_pv([(s_w, v_scr[g, lo:hi, :]), (s_c, vc)], sink=sink)
            o_ref[rs, cs] = o.astype(BF16)


def _swa(proj, sink, l, latent, caches=None, tabs=None):
    s_len = LAT_S if latent else CTX_S
    nb = N_LAT_B if latent else N_CTX_B
    rb = (N_CTX // s_len) if latent else 0
    in_specs = [pl.BlockSpec((s_len, 512), lambda b: (rb + b, COL["d_q"] // 4)),
                pl.BlockSpec((s_len, 256), lambda b: (rb + b, COL["d_k"] // 2)),
                pl.BlockSpec((s_len, 256), lambda b: (rb + b, COL["d_v"] // 2))]
    args = [proj, proj, proj]
    out_specs = [pl.BlockSpec((s_len, 512), lambda b: (b, 0))]
    out_shape = [jax.ShapeDtypeStruct((nb * s_len, 512), BF16)]
    if latent:
        in_specs += [pl.BlockSpec((1, 1, PAST, 256), lambda b: (b, l, 0, 0)),
                     pl.BlockSpec((1, 1, PAST, 256), lambda b: (b, l, 0, 0)),
                     _full((s_len, 128)), _full((s_len, 128)), _full((s_len, 128))]
        args += [caches[0], caches[1], *tabs]
    else:
        out_specs += [pl.BlockSpec((s_len, 256), lambda b: (b, 0))] * 2
        out_shape += [jax.ShapeDtypeStruct((nb * s_len, 256), F32)] * 2
    in_specs += [pl.BlockSpec(memory_space=pltpu.SMEM)]
    args += [sink]
    return pl.pallas_call(
        functools.partial(_swa_kernel, latent=latent, s_len=s_len),
        grid=(nb,), in_specs=in_specs, out_specs=out_specs, out_shape=out_shape,
        scratch_shapes=[pltpu.VMEM((2, s_len, HD), BF16), pltpu.VMEM((2, s_len, HD), BF16)],
        compiler_params=_params(("arbitrary",)),
        name="swa_latent" if latent else "swa_context",
    )(*args)


def _outproj_kernel(oa_ref, ob_ref, oc_ref, od_ref, x_ref, mod_ref, g_ref, w_ref, x1_ref, h2t_ref):
    acc = _dot(oa_ref[...], w_ref[0])
    acc = acc + _dot(ob_ref[...], w_ref[1])
    acc = acc + _dot(oc_ref[...], w_ref[2])
    acc = acc + _dot(od_ref[...], w_ref[3])
    m = mod_ref[0]
    x1 = x_ref[...] + m[2:3, :] * acc
    x1_ref[...] = x1
    y = x1 * lax.rsqrt(jnp.mean(x1 * x1, axis=-1, keepdims=True) + EPS) * g_ref[...]
    h2 = y * (1.0 + m[4:5, :]) + m[3:4, :]
    h2t_ref[...] = h2.T.astype(BF16)


def _outproj(os_, x, mod, g, w4):
    tm = 512
    row = lambda i: (i, 0)
    return pl.pallas_call(
        _outproj_kernel,
        grid=(N_TOK // tm,),
        in_specs=[pl.BlockSpec((tm, 512), row)] * 4 + [
            pl.BlockSpec((tm, D), row),
            pl.BlockSpec((1, N_MOD, D), lambda i: (_cond_of_rows(i * tm), 0, 0)),
            _full((1, D)), _full((4, 512, D))],
        out_specs=[pl.BlockSpec((tm, D), row), pl.BlockSpec((D, tm), lambda i: (0, i))],
        out_shape=[jax.ShapeDtypeStruct((N_TOK, D), F32), jax.ShapeDtypeStruct((D, N_TOK), BF16)],
        compiler_params=_params(("arbitrary",)),
        name="outproj",
    )(*os_, x, mod, g, w4)


PEER_TB = 512


def _peer_score_kernel(h2t_ref, wq_ref, sk_ref, s1_ref, s2_ref, e1_ref, e2_ref, thr_ref, q_scr, d_scr):
    q_scr[...] = _dot(wq_ref[...], h2t_ref[...]).astype(BF16)
    for h in range(PEER_HEADS):
        for p, dst in ((0, s1_ref), (1, s2_ref)):
            hp = 2 * h + p
            dst[h] = _dot(sk_ref[hp], q_scr[hp * N_KEYS:(hp + 1) * N_KEYS, :])

    iota = lax.broadcasted_iota(jnp.int32, (N_KEYS, LANE), 0).astype(F32)
    ninf = -jnp.inf

    def tile(i, carry):
        h = i // (PEER_TB // LANE)
        l0 = pl.multiple_of((i % (PEER_TB // LANE)) * LANE, LANE)
        lanes = pl.ds(l0, LANE)
        for p, src in ((0, s1_ref), (1, s2_ref)):
            cur = src[h, :, lanes]
            for r in range(PEER_TOPK):
                m = cur.max(axis=0, keepdims=True)
                d_scr[p, r:r + 1, :] = m
                first = jnp.where(cur == m, iota, float(N_KEYS)).min(axis=0, keepdims=True)
                cur = jnp.where(iota == first, ninf, cur)
        d1 = d_scr[0]
        d2 = d_scr[1]
        cand = [d1[a:a + 1, :] + d2 for a in range(PEER_TOPK)]
        top = d1[0:1, :] + d2[0:1, :]
        cur = list(cand)
        thr = jnp.full((1, LANE), ninf, F32)
        seen = jnp.zeros((1, LANE), F32)
        for r in range(PEER_TOPK):
            m = cur[0]
            for c in cur[1:]:
                m = jnp.maximum(m, c)
            m = m.max(axis=0, keepdims=True)
            thr = jnp.where(seen < float(PEER_TOPK), m, thr)
            cnt = None
            for a in range(PEER_TOPK):
                eq = cur[a] == m
                ca = jnp.where(eq, 1.0, 0.0)
                cnt = ca if cnt is None else cnt + ca
                cur[a] = jnp.where(eq, ninf, cur[a])
            seen = seen + cnt.sum(axis=0, keepdims=True)
        z = None
        for c in cand:
            t = jnp.where(c >= thr, jnp.exp(c - top), 0.0)
            z = t if z is None else z + t
        zinv = 1.0 / z.sum(axis=0, keepdims=True)
        thr_ref[h, :, lanes] = thr
        e1_ref[h, :, lanes] = jnp.exp(s1_ref[h, :, lanes] - d1[0:1, :]) * zinv
        e2_ref[h, :, lanes] = jnp.exp(s2_ref[h, :, lanes] - d2[0:1, :])
        return carry

    lax.fori_loop(0, PEER_HEADS * (PEER_TB // LANE), tile, 0)


def _peer_scores(h2t, wq_t, subkeys):
    tb = PEER_TB
    blk3 = pl.BlockSpec((PEER_HEADS, N_KEYS, tb), lambda i: (0, 0, i))
    sds3 = jax.ShapeDtypeStruct((PEER_HEADS, N_KEYS, N_TOK), F32)
    return pl.pallas_call(
        _peer_score_kernel,
        grid=(N_TOK // tb,),
        in_specs=[pl.BlockSpec((D, tb), lambda i: (0, i)), _full((D, D)),
                  _full((2 * PEER_HEADS, N_KEYS, N_KEYS))],
        out_specs=[blk3, blk3, blk3, blk3, pl.BlockSpec((PEER_HEADS, 1, tb), lambda i: (0, 0, i))],
        out_shape=[sds3, sds3, sds3, sds3, jax.ShapeDtypeStruct((PEER_HEADS, 1, N_TOK), F32)],
        scratch_shapes=[pltpu.VMEM((D, tb), BF16), pltpu.VMEM((2, PEER_TOPK, LANE), F32)],
        compiler_params=_params(("arbitrary",)),
        name="peer_scores",
    )(h2t, wq_t, subkeys)


PEER_EB = 1024


def _peer_main_kernel(h2t_ref, u_ref, vt_ref, s1_ref, s2_ref, e1_ref, e2_ref, thr_ref, x1_ref, mod_ref,
                      x2_ref, acc_ref, wt_ref):
    j = pl.program_id(1)

    @pl.when(j == 0)
    def _():
        acc_ref[...] = jnp.zeros_like(acc_ref)

    act_t = _dot(u_ref[...], h2t_ref[...])
    n_i1 = PEER_EB // N_KEYS
    for a in range(n_i1):
        for lt in range(PEER_TB // LANE):
            lanes = slice(lt * LANE, (lt + 1) * LANE)
            g = jnp.zeros((N_KEYS, LANE), F32)
            for h in range(PEER_HEADS):
                s1r = s1_ref[h, a:a + 1, lanes]
                e1r = e1_ref[h, a:a + 1, lanes]
                tot = s2_ref[h, :, lanes] + s1r
                g = g + jnp.where(tot >= thr_ref[h, :, lanes], e2_ref[h, :, lanes] * e1r, 0.0)
            act = act_t[a * N_KEYS:(a + 1) * N_KEYS, lanes]
            w = g * (0.5 * act * (1.0 + lax.erf(act * (2.0 ** -0.5))))
            wt_ref[a * N_KEYS:(a + 1) * N_KEYS, lanes] = w.astype(BF16)
    acc_ref[...] += _dot(vt_ref[...], wt_ref[...])

    @pl.when(j == pl.num_programs(1) - 1)
    def _():
        x2_ref[...] = x1_ref[...] + mod_ref[0][5:6, :] * acc_ref[...].T


def _peer_main(h2t, u_bf, vt_bf, stats, x1, mod):
    tb, eb = PEER_TB, PEER_EB
    n_exp = N_KEYS * N_KEYS
    s1, s2, e1, e2, thr = stats
    blk3 = pl.BlockSpec((PEER_HEADS, N_KEYS, tb), lambda i, j: (0, 0, i))
    row8 = pl.BlockSpec((PEER_HEADS, eb // N_KEYS, tb), lambda i, j: (0, j, i))
    assert eb // N_KEYS == 8
    return pl.pallas_call(
        _peer_main_kernel,
        grid=(N_TOK // tb, n_exp // eb),
        in_specs=[pl.BlockSpec((D, tb), lambda i, j: (0, i)),
                  pl.BlockSpec((eb, D), lambda i, j: (j, 0)),
                  pl.BlockSpec((D, eb), lambda i, j: (0, j)),
                  row8, blk3, row8, blk3,
                  pl.BlockSpec((PEER_HEADS, 1, tb), lambda i, j: (0, 0, i)),
                  pl.BlockSpec((tb, D), lambda i, j: (i, 0)),
                  pl.BlockSpec((1, N_MOD, D), lambda i, j: (_cond_of_rows(i * tb), 0, 0))],
        out_specs=pl.BlockSpec((tb, D), lambda i, j: (i, 0)),
        out_shape=jax.ShapeDtypeStruct((N_TOK, D), F32),
        scratch_shapes=[pltpu.VMEM((D, tb), F32), pltpu.VMEM((eb, tb), BF16)],
        compiler_params=_params(("arbitrary", "arbitrary")),
        name="peer_main",
    )(h2t, u_bf, vt_bf, s1, s2, e1, e2, thr, x1, mod)


def _final_norm_kernel(x_ref, g_ref, o_ref):
    x = x_ref[...]
    o_ref[...] = x * lax.rsqrt(jnp.mean(x * x, axis=-1, keepdims=True) + EPS) * g_ref[...]


def _final_norm(x, g, row_block0, n_rows):
    tm = 512
    return pl.pallas_call(
        _final_norm_kernel,
        grid=(n_rows // tm,),
        in_specs=[pl.BlockSpec((tm, D), lambda i: (row_block0 + i, 0)), _full((1, D))],
        out_specs=pl.BlockSpec((tm, D), lambda i: (i, 0)),
        out_shape=jax.ShapeDtypeStruct((n_rows, D), F32),
        compiler_params=_params(("arbitrary",)),
        name="final_norm",
    )(x, g)


def _pad_cols(w, n):
    return jnp.pad(w, ((0, 0), (0, n - w.shape[1])))


def _prep_w_in(w):
    pts = np.cumsum(np.array(IN_SPLITS))[:-1].tolist()
    q_lat, ckv, krope, na_q, na_k, na_v, c_q, c_k, c_v, d_q, d_k, d_v = jnp.split(w, pts, axis=-1)
    cols = [na_q, na_k, na_v, c_q, d_q, _pad_cols(q_lat, 512), ckv, c_k, c_v, d_k, d_v, _pad_cols(krope, 128),
            jnp.zeros((D, 128), w.dtype)]
    return jnp.concatenate(cols, axis=-1).astype(BF16)


def _prep_w_uq(w):
    w = w.reshape(MLA_Q_RANK, 4, HD + MLA_ROPE)
    nope = w[:, :, :HD].reshape(MLA_Q_RANK, 4 * HD)
    rope = jnp.pad(w[:, :, HD:], ((0, 0), (0, 0), (0, HD - MLA_ROPE))).reshape(MLA_Q_RANK, 4 * HD)
    return jnp.pad(jnp.concatenate([nope, rope], axis=-1), ((0, 512 - MLA_Q_RANK), (0, 0))).astype(BF16)


def _rope_tables():
    t = np.arange(LAT_S)
    rows, cols = (t // GRID_W).astype(np.float32), (t % GRID_W).astype(np.float32)

    def ang(d):
        n = d // 4
        freqs = jnp.asarray(ROPE_THETA, F32) ** (-jnp.arange(n, dtype=F32) / n)
        return jnp.concatenate([rows[:, None] * freqs, cols[:, None] * freqs], axis=-1)

    a = ang(MLA_ROPE)
    c, s, z = jnp.cos(a), jnp.sin(a), jnp.zeros_like(a)
    tab64 = (jnp.concatenate([c, c, z, z], -1), jnp.concatenate([-s, z, z, z], -1),
             jnp.concatenate([z, s, z, z], -1))
    a = ang(HD)
    c, s, z = jnp.cos(a), jnp.sin(a), jnp.zeros_like(a)
    tab128 = (jnp.concatenate([c, c], -1), jnp.concatenate([-s, z], -1), jnp.concatenate([z, s], -1))
    return tab64, tab128


def kernel(x_prompt, x_sample, c, cache_mla_ckv, cache_mla_krope, cache_na_k, cache_na_v, cache_gqa_k,
           cache_gqa_v, cache_swa_k, cache_swa_v, c_ctx, w_mod, b_mod, norm1_g, w_in, mla_q_norm_g, mla_w_uq,
           mla_kv_norm_g, mla_w_uk, mla_w_uv, na_rpb, gqa_q_norm_g, gqa_k_norm_g, swa_sink, w_out, norm2_g,
           peer_w_q, peer_subkeys, peer_u, peer_v, final_norm_g):
    x = jnp.concatenate([x_prompt.reshape(N_CTX, D), x_sample.reshape(N_LAT_B * LAT_S, D)], axis=0)
    cond8 = jnp.concatenate([c_ctx[None, :], c, jnp.zeros((3, D), F32)], axis=0)
    mod_all = _modulation(cond8, w_mod, b_mod).reshape(DEPTH, 8, N_MOD, D)
    tab64, tab128 = _rope_tables()
    ckr_p = jnp.pad(cache_mla_krope, ((0, 0), (0, 0), (0, 0), (0, HD - MLA_ROPE)))
    c_na_k = cache_na_k.reshape(N_LAT_B, DEPTH, PAST, 512)
    c_na_v = cache_na_v.reshape(N_LAT_B, DEPTH, PAST, 512)
    c_gqa_k = cache_gqa_k.reshape(N_LAT_B, DEPTH, PAST, 256)
    c_gqa_v = cache_gqa_v.reshape(N_LAT_B, DEPTH, PAST, 256)
    c_swa_k = cache_swa_k.reshape(N_LAT_B, DEPTH, PAST, 256)
    c_swa_v = cache_swa_v.reshape(N_LAT_B, DEPTH, PAST, 256)

    states = []
    for l in range(DEPTH):
        mod = mod_all[l]
        proj = _inproj(x, mod, norm1_g[l][None, :], _prep_w_in(w_in[l]))
        mla_w = (jnp.pad(mla_q_norm_g[l], (0, 512 - MLA_Q_RANK))[None, :], _prep_w_uq(mla_w_uq[l]),
                 mla_kv_norm_g[l][None, :], mla_w_uk[l].astype(BF16), mla_w_uv[l].astype(BF16))
        gqa_w = (gqa_q_norm_g[l][None, :], gqa_k_norm_g[l][None, :])
        oa_c, st_ckv, st_kr = _mla(proj, mla_w, l, False)
        oa_l, = _mla(proj, mla_w, l, True, (cache_mla_ckv, ckr_p), tab64)
        ob_c, st_nk, st_nv = _na(proj, l, False)
        ob_l, = _na(proj, l, True, (c_na_k, c_na_v), _na_bias(na_rpb[l]))
        oc_c, st_gk, st_gv = _gqa(proj, gqa_w, l, False)
        oc_l, = _gqa(proj, gqa_w, l, True, (c_gqa_k, c_gqa_v), tab128)
        od_c, st_sk, st_sv = _swa(proj, swa_sink[l], l, False)
        od_l, = _swa(proj, swa_sink[l], l, True, (c_swa_k, c_swa_v), tab128)
        os_ = [jnp.concatenate([a, b], axis=0) for a, b in ((oa_c, oa_l), (ob_c, ob_l), (oc_c, oc_l), (od_c, od_l))]
        x1, h2t = _outproj(os_, x, mod, norm2_g[l][None, :], w_out[l].astype(BF16).reshape(4, 512, D))
        stats = _peer_scores(h2t, peer_w_q[l].T.astype(BF16),
                             peer_subkeys[l].reshape(2 * PEER_HEADS, N_KEYS, N_KEYS).astype(BF16))
        x = _peer_main(h2t, peer_u[l].astype(BF16), peer_v[l].T.astype(BF16), stats, x1, mod)
        states.append((st_ckv, st_kr, st_nk, st_nv, st_gk, st_gv, st_sk, st_sv))

    g = final_norm_g[None, :]
    y_prompt = _final_norm(x, g, 0, N_CTX).reshape(N_CTX_B, CTX_S, D)
    y_sample = _final_norm(x, g, N_CTX // 512, N_LAT_B * LAT_S).reshape(N_LAT_B, LAT_S, D)
    tails = ((MLA_KV_RANK,), (MLA_ROPE,), (4, HD), (4, HD), (2, HD), (2, HD), (2, HD), (2, HD))
    outs = [jnp.stack([st[k].reshape((N_CTX_B, CTX_S) + tails[k]) for st in states], axis=1) for k in range(8)]
    return (y_prompt, y_sample, *outs)
```

```python
import functools

import numpy as np
import jax
import jax.numpy as jnp
from jax import lax
from jax.experimental import pallas as pl
from jax.experimental.pallas import tpu as pltpu

F32 = jnp.float32
BF16 = jnp.bfloat16

D = 2048
N_CTX_B, CTX_S = 16, 256
N_LAT_B, LAT_S = 4, 1024
N_CTX = N_CTX_B * CTX_S
N_TOK = N_CTX + N_LAT_B * LAT_S
DEPTH = 2
PAST = 512
GRID_W = 64
HD = 128
MLA_Q_RANK, MLA_KV_RANK, MLA_ROPE = 384, 256, 64
NA_ROWS, NA_COLS = 8, 16
SWA_W = 128
N_KEYS = 128
PEER_HEADS = 8
PEER_TOPK = 16
N_MOD = 6
EPS = 1e-6
NEG_INF = -1e30
ROPE_THETA = 10000.0
LANE = 128

IN_SPLITS = (384, 256, 64, 512, 512, 512, 512, 256, 256, 512, 256, 256)
D_IN_P = 4608
COL = dict(na_q=0, na_k=4, na_v=8, c_q=12, d_q=16, q_lat=20, ckv=24, c_k=26, c_v=28, d_k=30, d_v=32,
           krope=34)

VMEM_LIMIT = 56 * 1024 * 1024


def _params(sem):
    return pltpu.CompilerParams(dimension_semantics=sem, vmem_limit_bytes=VMEM_LIMIT)


def _full(shape):
    nd = len(shape)
    return pl.BlockSpec(shape, lambda *_: (0,) * nd)


def _dot(a, b):
    return jnp.dot(a, b, preferred_element_type=F32)


def _dot_nt(a, b):
    return lax.dot_general(a, b, (((1,), (1,)), ((), ())), preferred_element_type=F32)


def _cond_of_rows(row0):
    return jnp.where(row0 < N_CTX, 0, 1 + (row0 - N_CTX) // LAT_S)


def _mod_kernel(c_ref, w_ref, b_ref, o_ref):
    c = c_ref[...]
    s = c * jax.nn.sigmoid(c)
    o_ref[0] = _dot(s.astype(BF16), w_ref[0].astype(BF16)) + b_ref[0]


def _modulation(cond8, w_mod, b_mod):
    tn = 1536
    n = N_MOD * D
    return pl.pallas_call(
        _mod_kernel,
        grid=(DEPTH, n // tn),
        in_specs=[pl.BlockSpec((8, D), lambda l, j: (0, 0)),
                  pl.BlockSpec((1, D, tn), lambda l, j: (l, 0, j)),
                  pl.BlockSpec((1, 1, tn), lambda l, j: (l, 0, j))],
        out_specs=pl.BlockSpec((1, 8, tn), lambda l, j: (l, 0, j)),
        out_shape=jax.ShapeDtypeStruct((DEPTH, 8, n), F32),
        compiler_params=_params(("arbitrary", "arbitrary")),
        name="modulation",
    )(cond8, w_mod, b_mod.reshape(DEPTH, 1, n))


def _inproj_kernel(x_ref, mod_ref, g_ref, w_ref, o_ref):
    x = x_ref[...]
    y = x * lax.rsqrt(jnp.mean(x * x, axis=-1, keepdims=True) + EPS) * g_ref[...]
    m = mod_ref[0]
    h = y * (1.0 + m[1:2, :]) + m[0:1, :]
    o_ref[...] = _dot(h.astype(BF16), w_ref[...])


def _inproj(x, mod, g, w_p):
    tm, tn = 512, 2304
    return pl.pallas_call(
        _inproj_kernel,
        grid=(D_IN_P // tn, N_TOK // tm),
        in_specs=[pl.BlockSpec((tm, D), lambda j, i: (i, 0)),
                  pl.BlockSpec((1, N_MOD, D), lambda j, i: (_cond_of_rows(i * tm), 0, 0)),
                  pl.BlockSpec((1, D), lambda j, i: (0, 0)),
                  pl.BlockSpec((D, tn), lambda j, i: (0, j))],
        out_specs=pl.BlockSpec((tm, tn), lambda j, i: (i, j)),
        out_shape=jax.ShapeDtypeStruct((N_TOK, D_IN_P), F32),
        compiler_params=_params(("arbitrary", "arbitrary")),
        name="inproj",
    )(x, mod, g, w_p)


def _rms(x, g, n=None):
    n = x.shape[-1] if n is None else n
    return x * lax.rsqrt(jnp.sum(x * x, axis=-1, keepdims=True) * (1.0 / n) + EPS) * g


def _rope(x, cos, sin_lo, sin_hi, shift):
    return (x * cos + pltpu.roll(x, LANE - shift, 1) * sin_lo + pltpu.roll(x, shift, 1) * sin_hi)


def _softmax_pv(parts, sink=None):
    m = parts[0][0].max(axis=-1, keepdims=True)
    for s, _ in parts[1:]:
        m = jnp.maximum(m, s.max(axis=-1, keepdims=True))
    if sink is not None:
        m = jnp.maximum(m, sink)
    es = [jnp.exp(s - m) for s, _ in parts]
    den = es[0].sum(axis=-1, keepdims=True)
    for e in es[1:]:
        den = den + e.sum(axis=-1, keepdims=True)
    if sink is not None:
        den = den + jnp.exp(sink - m)
    inv = 1.0 / den
    o = None
    for e, (_, v) in zip(es, parts):
        t = _dot((e * inv).astype(BF16), v)
        o = t if o is None else o + t
    return o


QB = 256


def _mla_kernel(*refs, latent, s_len):
    if latent:
        (ql_ref, ckv_ref, kr_ref, cckv_ref, ckr_ref, cos_ref, sa_ref, sb_ref,
         qg_ref, wuq_ref, kvg_ref, wuk_ref, wuv_ref, o_ref, ckv_scr, q_scr, k_scr, v_scr) = refs
    else:
        (ql_ref, ckv_ref, kr_ref, qg_ref, wuq_ref, kvg_ref, wuk_ref, wuv_ref,
         o_ref, ckv_out, kr_out, ckv_scr, q_scr, k_scr, v_scr) = refs
    p_len = PAST if latent else 0
    qn = _rms(ql_ref[...], qg_ref[...], MLA_Q_RANK).astype(BF16)
    ckv_n = _rms(ckv_ref[...], kvg_ref[...])
    kr = kr_ref[...]
    if latent:
        cos, sa, sb = cos_ref[...], sa_ref[...], sb_ref[...]
        kr = _rope(kr, cos, sa, sb, 32)
        ckv_scr[0:p_len, :] = cckv_ref[0, 0].astype(BF16)
        kr_all = jnp.concatenate([ckr_ref[0, 0], kr], axis=0)
    else:
        ckv_out[...] = ckv_n
        kr_out[...] = kr[:, :MLA_ROPE]
        kr_all = kr
    ckv_scr[p_len:, :] = ckv_n.astype(BF16)
    kr_all = kr_all.astype(BF16)
    ckv_all = ckv_scr[...]
    for h in range(4):
        q_scr[h, :, 0:HD] = _dot(qn, wuq_ref[:, h * HD:(h + 1) * HD]).astype(BF16)
        qr = _dot(qn, wuq_ref[:, (4 + h) * HD:(5 + h) * HD])
        if latent:
            qr = _rope(qr, cos, sa, sb, 32)
        q_scr[h, :, HD:2 * HD] = qr.astype(BF16)
        k_scr[h, :, 0:HD] = _dot(ckv_all, wuk_ref[:, h * HD:(h + 1) * HD]).astype(BF16)
        k_scr[h, :, HD:2 * HD] = kr_all
        v_scr[h] = _dot(ckv_all, wuv_ref[:, h * HD:(h + 1) * HD]).astype(BF16)
    scale = float(HD + MLA_ROPE) ** -0.5
    for h in range(4):
        for qb in range(s_len // QB):
            s = _dot_nt(q_scr[h, qb * QB:(qb + 1) * QB, :], k_scr[h]) * scale
            o = _softmax_pv([(s, v_scr[h])])
            o_ref[qb * QB:(qb + 1) * QB, h * HD:(h + 1) * HD] = o.astype(BF16)


def _mla(proj, wts, l, latent, caches=None, tabs=None):
    s_len = LAT_S if latent else CTX_S
    nb = N_LAT_B if latent else N_CTX_B
    rb = (N_CTX // s_len) if latent else 0
    t_len = s_len + (PAST if latent else 0)
    qg, wuq, kvg, wuk, wuv = wts
    in_specs = [pl.BlockSpec((s_len, 512), lambda b: (rb + b, COL["q_lat"] // 4)),
                pl.BlockSpec((s_len, 256), lambda b: (rb + b, COL["ckv"] // 2)),
                pl.BlockSpec((s_len, 128), lambda b: (rb + b, COL["krope"]))]
    args = [proj, proj, proj]
    if latent:
        in_specs += [pl.BlockSpec((1, 1, PAST, 256), lambda b: (b, l, 0, 0)),
                     pl.BlockSpec((1, 1, PAST, 128), lambda b: (b, l, 0, 0)),
                     _full((s_len, 128)), _full((s_len, 128)), _full((s_len, 128))]
        args += [caches[0], caches[1], *tabs]
    in_specs += [_full((1, 512)), _full((512, 1024)), _full((1, 256)), _full((256, 512)), _full((256, 512))]
    args += [qg, wuq, kvg, wuk, wuv]
    out_specs = [pl.BlockSpec((s_len, 512), lambda b: (b, 0))]
    out_shape = [jax.ShapeDtypeStruct((nb * s_len, 512), BF16)]
    if not latent:
        out_specs += [pl.BlockSpec((s_len, 256), lambda b: (b, 0)), pl.BlockSpec((s_len, 64), lambda b: (b, 0))]
        out_shape += [jax.ShapeDtypeStruct((nb * s_len, 256), F32), jax.ShapeDtypeStruct((nb * s_len, 64), F32)]
    return pl.pallas_call(
        functools.partial(_mla_kernel, latent=latent, s_len=s_len),
        grid=(nb,), in_specs=in_specs, out_specs=out_specs, out_shape=out_shape,
        scratch_shapes=[pltpu.VMEM((t_len, 256), BF16), pltpu.VMEM((4, s_len, 256), BF16),
                        pltpu.VMEM((4, t_len, 256), BF16), pltpu.VMEM((4, t_len, HD), BF16)],
        compiler_params=_params(("arbitrary",)),
        name="mla_latent" if latent else "mla_context",
    )(*args)


def _na_kernel(*refs, latent, s_len):
    if latent:
        q_ref, k_ref, v_ref, ck_ref, cv_ref, bias_ref, o_ref = refs
    else:
        q_ref, k_ref, v_ref, o_ref, k_out, v_out = refs
        k_out[...] = k_ref[...]
        v_out[...] = v_ref[...]
    scale = float(HD) ** -0.5
    rows = s_len // GRID_W
    for h in range(4):
        cs = slice(h * HD, (h + 1) * HD)
        if not latent:
            s = _dot_nt(q_ref[:, cs].astype(BF16), k_ref[:, cs].astype(BF16)) * scale
            o_ref[:, cs] = _softmax_pv([(s, v_ref[:, cs].astype(BF16))]).astype(BF16)
            continue
        kc = ck_ref[0, 0, :, cs].astype(BF16)
        vc = cv_ref[0, 0, :, cs].astype(BF16)
        for r in range(rows):
            kr0 = min(max(r - NA_ROWS // 2, 0), rows - NA_ROWS)
            win = slice(kr0 * GRID_W, (kr0 + NA_ROWS) * GRID_W)
            q = q_ref[r * GRID_W:(r + 1) * GRID_W, cs].astype(BF16)
            s_nb = _dot_nt(q, k_ref[win, cs].astype(BF16)) * scale + bias_ref[h, kr0 - r + NA_ROWS - 1]
            s_c = _dot_nt(q, kc) * scale
            o = _softmax_pv([(s_nb, v_ref[win, cs].astype(BF16)), (s_c, vc)])
            o_ref[r * GRID_W:(r + 1) * GRID_W, cs] = o.astype(BF16)


def _na(proj, l, latent, caches=None, bias=None):
    s_len = LAT_S if latent else CTX_S
    nb = N_LAT_B if latent else N_CTX_B
    rb = (N_CTX // s_len) if latent else 0
    in_specs = [pl.BlockSpec((s_len, 512), lambda b: (rb + b, COL["na_q"] // 4)),
                pl.BlockSpec((s_len, 512), lambda b: (rb + b, COL["na_k"] // 4)),
                pl.BlockSpec((s_len, 512), lambda b: (rb + b, COL["na_v"] // 4))]
    args = [proj, proj, proj]
    out_specs = [pl.BlockSpec((s_len, 512), lambda b: (b, 0))]
    out_shape = [jax.ShapeDtypeStruct((nb * s_len, 512), BF16)]
    if latent:
        in_specs += [pl.BlockSpec((1, 1, PAST, 512), lambda b: (b, l, 0, 0)),
                     pl.BlockSpec((1, 1, PAST, 512), lambda b: (b, l, 0, 0)),
                     _full(bias.shape)]
        args += [caches[0], caches[1], bias]
    else:
        out_specs += [pl.BlockSpec((s_len, 512), lambda b: (b, 0))] * 2
        out_shape += [jax.ShapeDtypeStruct((nb * s_len, 512), F32)] * 2
    return pl.pallas_call(
        functools.partial(_na_kernel, latent=latent, s_len=s_len),
        grid=(nb,), in_specs=in_specs, out_specs=out_specs, out_shape=out_shape,
        compiler_params=_params(("arbitrary",)),
        name="na_latent" if latent else "na_context",
    )(*args)


def _na_bias(rpb):
    w = GRID_W
    side = w - NA_COLS
    p = jnp.pad(rpb, ((0, 0), (0, 0), (side, side + 1)))
    t = jnp.tile(p, (1, 1, w))[..., :w * (2 * w - 1)].reshape(4, 2 * NA_ROWS - 1, w, 2 * w - 1)[..., w - 1:]
    qc = np.arange(w)
    cs = np.clip(qc - NA_COLS // 2, 0, w - NA_COLS)
    valid = (qc[None, :] >= cs[:, None]) & (qc[None, :] < cs[:, None] + NA_COLS)
    t = jnp.where(valid[None, None], t, NEG_INF)
    return jnp.stack([jnp.concatenate([t[:, d0 + j] for j in range(NA_ROWS)], axis=-1)
                      for d0 in range(NA_ROWS)], axis=1).astype(F32)


def _gqa_kernel(*refs, latent, s_len):
    if latent:
        q_ref, k_ref, v_ref, ck_ref, cv_ref, cos_ref, sl_ref, sh_ref, qg_ref, kg_ref, o_ref, k_scr, v_scr = refs
        cos, sl, sh = cos_ref[...], sl_ref[...], sh_ref[...]
    else:
        q_ref, k_ref, v_ref, qg_ref, kg_ref, o_ref, k_out, v_out, k_scr, v_scr = refs
        v_out[...] = v_ref[...]
    p_len = PAST if latent else 0
    scale = float(HD) ** -0.5
    for g in range(2):
        cs = slice(g * HD, (g + 1) * HD)
        kn = _rms(k_ref[:, cs], kg_ref[...])
        if latent:
            kn = _rope(kn, cos, sl, sh, 64)
            k_scr[g, 0:p_len, :] = ck_ref[0, 0, :, cs].astype(BF16)
            v_scr[g, 0:p_len, :] = cv_ref[0, 0, :, cs].astype(BF16)
        else:
            k_out[:, cs] = kn
        k_scr[g, p_len:, :] = kn.astype(BF16)
        v_scr[g, p_len:, :] = v_ref[:, cs].astype(BF16)
    for hq in range(4):
        g = hq // 2
        cs = slice(hq * HD, (hq + 1) * HD)
        for qb in range(s_len // QB):
            rs = slice(qb * QB, (qb + 1) * QB)
            q = _rms(q_ref[rs, cs], qg_ref[...])
            if latent:
                q = _rope(q, cos[rs], sl[rs], sh[rs], 64)
            s = _dot_nt(q.astype(BF16), k_scr[g]) * scale
            o_ref[rs, cs] = _softmax_pv([(s, v_scr[g])]).astype(BF16)


def _gqa(proj, wts, l, latent, caches=None, tabs=None):
    s_len = LAT_S if latent else CTX_S
    nb = N_LAT_B if latent else N_CTX_B
    rb = (N_CTX // s_len) if latent else 0
    t_len = s_len + (PAST if latent else 0)
    in_specs = [pl.BlockSpec((s_len, 512), lambda b: (rb + b, COL["c_q"] // 4)),
                pl.BlockSpec((s_len, 256), lambda b: (rb + b, COL["c_k"] // 2)),
                pl.BlockSpec((s_len, 256), lambda b: (rb + b, COL["c_v"] // 2))]
    args = [proj, proj, proj]
    out_specs = [pl.BlockSpec((s_len, 512), lambda b: (b, 0))]
    out_shape = [jax.ShapeDtypeStruct((nb * s_len, 512), BF16)]
    if latent:
        in_specs += [pl.BlockSpec((1, 1, PAST, 256), lambda b: (b, l, 0, 0)),
                     pl.BlockSpec((1, 1, PAST, 256), lambda b: (b, l, 0, 0)),
                     _full((s_len, 128)), _full((s_len, 128)), _full((s_len, 128))]
        args += [caches[0], caches[1], *tabs]
    else:
        out_specs += [pl.BlockSpec((s_len, 256), lambda b: (b, 0))] * 2
        out_shape += [jax.ShapeDtypeStruct((nb * s_len, 256), F32)] * 2
    in_specs += [_full((1, HD)), _full((1, HD))]
    args += list(wts)
    return pl.pallas_call(
        functools.partial(_gqa_kernel, latent=latent, s_len=s_len),
        grid=(nb,), in_specs=in_specs, out_specs=out_specs, out_shape=out_shape,
        scratch_shapes=[pltpu.VMEM((2, t_len, HD), BF16), pltpu.VMEM((2, t_len, HD), BF16)],
        compiler_params=_params(("arbitrary",)),
        name="gqa_latent" if latent else "gqa_context",
    )(*args)


def _swa_kernel(*refs, latent, s_len):
    if latent:
        q_ref, k_ref, v_ref, ck_ref, cv_ref, cos_ref, sl_ref, sh_ref, sink_ref, o_ref, k_scr, v_scr = refs
        cos, sl, sh = cos_ref[...], sl_ref[...], sh_ref[...]
    else:
        q_ref, k_ref, v_ref, sink_ref, o_ref, k_out, v_out, k_scr, v_scr = refs
        k_out[...] = k_ref[...]
        v_out[...] = v_ref[...]
    scale = float(HD) ** -0.5
    for g in range(2):
        cs = slice(g * HD, (g + 1) * HD)
        k = k_ref[:, cs]
        if latent:
            k = _rope(k, cos, sl, sh, 64)
        k_scr[g] = k.astype(BF16)
        v_scr[g] = v_ref[:, cs].astype(BF16)
    for hq in range(4):
        g = hq // 2
        cs = slice(hq * HD, (hq + 1) * HD)
        gs = slice(g * HD, (g + 1) * HD)
        sink = sink_ref[hq]
        if not latent:
            s = _dot_nt(q_ref[:, cs].astype(BF16), k_scr[g]) * scale
            o_ref[:, cs] = _softmax_pv([(s, v_scr[g])], sink=sink).astype(BF16)
            continue
        kc = ck_ref[0, 0, :, gs].astype(BF16)
        vc = cv_ref[0, 0, :, gs].astype(BF16)
        for n in range(s_len // SWA_W):
            rs = slice(n * SWA_W, (n + 1) * SWA_W)
            lo, hi = max(0, (n - 1) * SWA_W), min(s_len, (n + 2) * SWA_W)
            q = _rope(q_ref[rs, cs], cos[rs], sl[rs], sh[rs], 64).astype(BF16)
            s_w = _dot_nt(q, k_scr[g, lo:hi, :]) * scale
            qpos = n * SWA_W + lax.broadcasted_iota(jnp.int32, s_w.shape, 0)
            kpos = lo + lax.broadcasted_iota(jnp.int32, s_w.shape, 1)
            s_w = jnp.where(jnp.abs(kpos - qpos) <= SWA_W, s_w, NEG_INF)
            s_c = _dot_nt(q, kc) * scale
            o = _softmax_pv([(s_w, v_scr[g, lo:hi, :]), (s_c, vc)], sink=sink)
            o_ref[rs, cs] = o.astype(BF16)


def _swa(proj, sink, l, latent, caches=None, tabs=None):
    s_len = LAT_S if latent else CTX_S
    nb = N_LAT_B if latent else N_CTX_B
    rb = (N_CTX // s_len) if latent else 0
    in_specs = [pl.BlockSpec((s_len, 512), lambda b: (rb + b, COL["d_q"] // 4)),
                pl.BlockSpec((s_len, 256), lambda b: (rb + b, COL["d_k"] // 2)),
                pl.BlockSpec((s_len, 256), lambda b: (rb + b, COL["d_v"] // 2))]
    args = [proj, proj, proj]
    out_specs = [pl.BlockSpec((s_len, 512), lambda b: (b, 0))]
    out_shape = [jax.ShapeDtypeStruct((nb * s_len, 512), BF16)]
    if latent:
        in_specs += [pl.BlockSpec((1, 1, PAST, 256), lambda b: (b, l, 0, 0)),
                     pl.BlockSpec((1, 1, PAST, 256), lambda b: (b, l, 0, 0)),
                     _full((s_len, 128)), _full((s_len, 128)), _full((s_len, 128))]
        args += [caches[0], caches[1], *tabs]
    else:
        out_specs += [pl.BlockSpec((s_len, 256), lambda b: (b, 0))] * 2
        out_shape += [jax.ShapeDtypeStruct((nb * s_len, 256), F32)] * 2
    in_specs += [pl.BlockSpec(memory_space=pltpu.SMEM)]
    args += [sink]
    return pl.pallas_call(
        functools.partial(_swa_kernel, latent=latent, s_len=s_len),
        grid=(nb,), in_specs=in_specs, out_specs=out_specs, out_shape=out_shape,
        scratch_shapes=[pltpu.VMEM((2, s_len, HD), BF16), pltpu.VMEM((2, s_len, HD), BF16)],
        compiler_params=_params(("arbitrary",)),
        name="swa_latent" if latent else "swa_context",
    )(*args)


def _outproj_kernel(oa_ref, ob_ref, oc_ref, od_ref, x_ref, mod_ref, g_ref, w_ref, x1_ref, h2t_ref):
    acc = _dot(oa_ref[...], w_ref[0])
    acc = acc + _dot(ob_ref[...], w_ref[1])
    acc = acc + _dot(oc_ref[...], w_ref[2])
    acc = acc + _dot(od_ref[...], w_ref[3])
    m = mod_ref[0]
    x1 = x_ref[...] + m[2:3, :] * acc
    x1_ref[...] = x1
    y = x1 * lax.rsqrt(jnp.mean(x1 * x1, axis=-1, keepdims=True) + EPS) * g_ref[...]
    h2 = y * (1.0 + m[4:5, :]) + m[3:4, :]
    for i in range(h2t_ref.shape[0]):
        h2t_ref[i] = h2[i * LANE:(i + 1) * LANE, :].T.astype(BF16)


def _outproj(os_, x, mod, g, w4):
    tm = 512
    row = lambda i: (i, 0)
    return pl.pallas_call(
        _outproj_kernel,
        grid=(N_TOK // tm,),
        in_specs=[pl.BlockSpec((tm, 512), row)] * 4 + [
            pl.BlockSpec((tm, D), row),
            pl.BlockSpec((1, N_MOD, D), lambda i: (_cond_of_rows(i * tm), 0, 0)),
            _full((1, D)), _full((4, 512, D))],
        out_specs=[pl.BlockSpec((tm, D), row), pl.BlockSpec((tm // LANE, D, LANE), lambda i: (i, 0, 0))],
        out_shape=[jax.ShapeDtypeStruct((N_TOK, D), F32), jax.ShapeDtypeStruct((N_TOK // LANE, D, LANE), BF16)],
        compiler_params=_params(("arbitrary",)),
        name="outproj",
    )(*os_, x, mod, g, w4)


PEER_TB = 512


def _lane_tiles(ref):
    return jnp.concatenate([ref[i] for i in range(ref.shape[0])], axis=1)


def _cmp_exchange(vals, i, j):
    a, b = vals[i], vals[j]
    if b is None:
        return
    if a is None:
        vals[i], vals[j] = b, None
    else:
        vals[i], vals[j] = jnp.maximum(a, b), jnp.minimum(a, b)


def _bitonic_merge16(vals):
    for j in (8, 4, 2, 1):
        for i in range(16):
            if i ^ j > i:
                _cmp_exchange(vals, i, i ^ j)
    return vals


def _sort16_desc(vals):
    vals = list(vals)
    for k in (2, 4, 8, 16):
        j = k // 2
        while j >= 1:
            for i in range(16):
                l = i ^ j
                if l > i:
                    _cmp_exchange(vals, *((i, l) if (i & k) == 0 else (l, i)))
            j //= 2
    return vals


def _merge_top16(a, b):
    c = []
    for i in range(16):
        x, y = a[i], b[15 - i]
        c.append(y if x is None else x if y is None else jnp.maximum(x, y))
    return _bitonic_merge16(c)


def _top16_of_keys(x):
    vals = _sort16_desc([x[v * 8:(v + 1) * 8] for v in range(N_KEYS // 8)])
    for shift in (4, 2, 1):
        vals = _merge_top16(vals, [pltpu.roll(v, shift, 0) for v in vals])
    return vals


def _top16_pair_sums(d1, d2):
    pad = lambda lst: lst + [None] * (16 - len(lst))
    cur = pad([d1[0] + d2[b] for b in range(16)])
    for a in range(1, 8):
        cur = _merge_top16(cur, pad([d1[a] + d2[b] for b in range(16 // (a + 1))]))
    return _merge_top16(cur, pad([d1[a] + d2[0] for a in range(8, 16)]))


def _peer_score_kernel(h2t_ref, wq_ref, sk_ref, s1_ref, s2_ref, e1_ref, e2_ref, thr_ref, q_scr):
    n_lt = PEER_TB // LANE
    q_scr[...] = _dot(wq_ref[...], _lane_tiles(h2t_ref)).astype(BF16)
    for h in range(PEER_HEADS):
        for p, dst in ((0, s1_ref), (1, s2_ref)):
            hp = 2 * h + p
            s = _dot(sk_ref[hp], q_scr[hp * N_KEYS:(hp + 1) * N_KEYS, :])
            for lt in range(n_lt):
                dst[h, lt] = s[:, lt * LANE:(lt + 1) * LANE]

    def tile(i, carry):
        h = i // n_lt
        lt = i % n_lt
        s1 = s1_ref[h, lt]
        s2 = s2_ref[h, lt]
        d1 = _top16_of_keys(s1)
        d2 = _top16_of_keys(s2)
        best = _top16_pair_sums(d1, d2)
        z = jnp.ones_like(best[0])
        for c in best[1:]:
            z = z + jnp.exp(c - best[0])
        zinv = 1.0 / z
        thr_ref[h, lt] = best[15][0:1, :]
        e1_ref[h, lt] = jnp.exp(s1 - d1[0][0:1, :]) * zinv[0:1, :]
        e2_ref[h, lt] = jnp.exp(s2 - d2[0][0:1, :])
        return carry

    lax.fori_loop(0, PEER_HEADS * n_lt, tile, 0)


def _peer_scores(h2t, wq_t, subkeys):
    tb = PEER_TB
    n_lt = tb // LANE
    tiles = pl.BlockSpec((PEER_HEADS, n_lt, N_KEYS, LANE), lambda i: (0, i, 0, 0))
    sds = jax.ShapeDtypeStruct((PEER_HEADS, N_TOK // LANE, N_KEYS, LANE), F32)
    return pl.pallas_call(
        _peer_score_kernel,
        grid=(N_TOK // tb,),
        in_specs=[pl.BlockSpec((n_lt, D, LANE), lambda i: (i, 0, 0)), _full((D, D)),
                  _full((2 * PEER_HEADS, N_KEYS, N_KEYS))],
        out_specs=[tiles, tiles, tiles, tiles, pl.BlockSpec((PEER_HEADS, n_lt, 1, LANE), lambda i: (0, i, 0, 0))],
        out_shape=[sds, sds, sds, sds, jax.ShapeDtypeStruct((PEER_HEADS, N_TOK // LANE, 1, LANE), F32)],
        scratch_shapes=[pltpu.VMEM((D, tb), BF16)],
        compiler_params=_params(("arbitrary",)),
        name="peer_scores",
    )(h2t, wq_t, subkeys)


PEER_EB = 1024


N_CHUNK = N_KEYS * N_KEYS // PEER_EB
N_BLK = N_TOK // PEER_TB
N_WORK = N_BLK * N_CHUNK
I1_PER_CHUNK = PEER_EB // N_KEYS
A_PER_IT = 4
KEY_ROWS = 32
V_ROWS = D * A_PER_IT // I1_PER_CHUNK


def _peer_main_kernel(h2t_ref, u_ref, vt_ref, s1_ref, s2_ref, e1_ref, e2_ref, thr_ref, x1_ref, mod_ref,
                      x2_ref, acc_ref, act0, act1, wt0, wt1):
    s = pl.program_id(0)
    chunk_c = jnp.clip(s - 2, 0, N_WORK - 1) % N_CHUNK
    keep = jnp.where(chunk_c == 0, 0.0, 1.0)

    @pl.when(s == 0)
    def _():
        for ref in (act0, act1, wt0, wt1, acc_ref):
            ref[...] = jnp.zeros_like(ref)

    def stages(act_new, act_cur, wt_cur, wt_old):
        def body(it, carry):
            a0 = it * A_PER_IT

            def gate_tile(lt, kq):
                lanes = slice(lt * LANE, (lt + 1) * LANE)
                keys = slice(kq * KEY_ROWS, (kq + 1) * KEY_ROWS)
                g = [jnp.zeros((KEY_ROWS, LANE), F32) for _ in range(A_PER_IT)]
                for h in range(PEER_HEADS):
                    s2 = s2_ref[h, lt, keys, :]
                    e2 = e2_ref[h, lt, keys, :]
                    thr = thr_ref[h, lt]
                    for k in range(A_PER_IT):
                        tot = s2 + s1_ref[a0 + k, h:h + 1, lanes]
                        gate = e2 * e1_ref[a0 + k, h:h + 1, lanes]
                        g[k] = g[k] + jnp.where(tot >= thr, gate, 0.0)
                for k in range(A_PER_IT):
                    r = pl.ds(pl.multiple_of((a0 + k) * N_KEYS + kq * KEY_ROWS, KEY_ROWS), KEY_ROWS)
                    act = act_cur[lt, r, :]
                    w = g[k] * (0.5 * act * (1.0 + lax.erf(act * (2.0 ** -0.5))))
                    wt_cur[lt, r, :] = w.astype(BF16)

            def project(m, n):
                rows = pl.ds(pl.multiple_of((a0 + m) * N_KEYS, N_KEYS), N_KEYS)
                rhs = jnp.concatenate([h2t_ref[2 * n], h2t_ref[2 * n + 1]], axis=1)
                res = _dot(u_ref[rows, :], rhs)
                act_new[2 * n, rows, :] = res[:, :LANE]
                act_new[2 * n + 1, rows, :] = res[:, LANE:]

            def fold(m, n):
                vr = V_ROWS // A_PER_IT
                vrows = pl.ds(pl.multiple_of(it * V_ROWS + m * vr, vr), vr)
                cols = slice(2 * n * LANE, (2 * n + 2) * LANE)
                rhs = jnp.concatenate([wt_old[2 * n], wt_old[2 * n + 1]], axis=1)
                acc_ref[vrows, cols] = acc_ref[vrows, cols] * keep + _dot(vt_ref[vrows, :], rhs)

            mxu = [functools.partial(f, m, n) for f in (project, fold) for m in range(A_PER_IT) for n in range(2)]
            for i in range(len(mxu)):
                mxu[i]()
                gate_tile(i // (N_KEYS // KEY_ROWS), i % (N_KEYS // KEY_ROWS))
            return carry
        return body

    @pl.when(s % 2 == 0)
    def _():
        lax.fori_loop(0, I1_PER_CHUNK // A_PER_IT, stages(act0, act1, wt1, wt0), 0)

    @pl.when(s % 2 == 1)
    def _():
        lax.fori_loop(0, I1_PER_CHUNK // A_PER_IT, stages(act1, act0, wt0, wt1), 0)

    @pl.when(chunk_c == N_CHUNK - 1)
    def _():
        x2_ref[...] = x1_ref[...] + mod_ref[0][5:6, :] * acc_ref[...].T


def _peer_main(h2t, u_bf, vt_bf, stats, x1, mod):
    tb, eb = PEER_TB, PEER_EB
    s1x, s2, e1x, e2, thr = stats
    item_a = lambda s: jnp.minimum(s, N_WORK - 1)
    item_b = lambda s: jnp.clip(s - 1, 0, N_WORK - 1)
    item_c = lambda s: jnp.clip(s - 2, 0, N_WORK - 1)
    once = pl.Buffered(1)
    n_lt = tb // LANE
    tiles = pl.BlockSpec((PEER_HEADS, n_lt, N_KEYS, LANE), lambda s: (0, item_b(s) // N_CHUNK, 0, 0),
                         pipeline_mode=once)
    row8 = pl.BlockSpec((I1_PER_CHUNK, PEER_HEADS, tb), lambda s: (item_b(s) % N_CHUNK, 0, item_b(s) // N_CHUNK))
    return pl.pallas_call(
        _peer_main_kernel,
        grid=(N_WORK + 2,),
        in_specs=[pl.BlockSpec((n_lt, D, LANE), lambda s: (item_a(s) // N_CHUNK, 0, 0)),
                  pl.BlockSpec((eb, D), lambda s: (item_a(s) % N_CHUNK, 0)),
                  pl.BlockSpec((D, eb), lambda s: (0, item_c(s) % N_CHUNK)),
                  row8, tiles, row8, tiles,
                  pl.BlockSpec((PEER_HEADS, n_lt, 1, LANE), lambda s: (0, item_b(s) // N_CHUNK, 0, 0)),
                  pl.BlockSpec((tb, D), lambda s: (item_c(s) // N_CHUNK, 0), pipeline_mode=once),
                  pl.BlockSpec((1, N_MOD, D), lambda s: (_cond_of_rows((item_c(s) // N_CHUNK) * tb), 0, 0))],
        out_specs=pl.BlockSpec((tb, D), lambda s: (item_c(s) // N_CHUNK, 0)),
        out_shape=jax.ShapeDtypeStruct((N_TOK, D), F32),
        scratch_shapes=[pltpu.VMEM((D, tb), F32),
                        pltpu.VMEM((n_lt, eb, LANE), F32), pltpu.VMEM((n_lt, eb, LANE), F32),
                        pltpu.VMEM((n_lt, eb, LANE), BF16), pltpu.VMEM((n_lt, eb, LANE), BF16)],
        compiler_params=_params(("arbitrary",)),
        name="peer_main",
    )(h2t, u_bf, vt_bf, s1x, s2, e1x, e2, thr, x1, mod)


def _final_norm_kernel(x_ref, g_ref, o_ref):
    x = x_ref[...]
    o_ref[...] = x * lax.rsqrt(jnp.mean(x * x, axis=-1, keepdims=True) + EPS) * g_ref[...]


def _final_norm(x, g, row_block0, n_rows):
    tm = 512
    return pl.pallas_call(
        _final_norm_kernel,
        grid=(n_rows // tm,),
        in_specs=[pl.BlockSpec((tm, D), lambda i: (row_block0 + i, 0)), _full((1, D))],
        out_specs=pl.BlockSpec((tm, D), lambda i: (i, 0)),
        out_shape=jax.ShapeDtypeStruct((n_rows, D), F32),
        compiler_params=_params(("arbitrary",)),
        name="final_norm",
    )(x, g)


def _pad_cols(w, n):
    return jnp.pad(w, ((0, 0), (0, n - w.shape[1])))


def _prep_w_in(w):
    pts = np.cumsum(np.array(IN_SPLITS))[:-1].tolist()
    q_lat, ckv, krope, na_q, na_k, na_v, c_q, c_k, c_v, d_q, d_k, d_v = jnp.split(w, pts, axis=-1)
    cols = [na_q, na_k, na_v, c_q, d_q, _pad_cols(q_lat, 512), ckv, c_k, c_v, d_k, d_v, _pad_cols(krope, 128),
            jnp.zeros((D, 128), w.dtype)]
    return jnp.concatenate(cols, axis=-1).astype(BF16)


def _prep_w_uq(w):
    w = w.reshape(MLA_Q_RANK, 4, HD + MLA_ROPE)
    nope = w[:, :, :HD].reshape(MLA_Q_RANK, 4 * HD)
    rope = jnp.pad(w[:, :, HD:], ((0, 0), (0, 0), (0, HD - MLA_ROPE))).reshape(MLA_Q_RANK, 4 * HD)
    return jnp.pad(jnp.concatenate([nope, rope], axis=-1), ((0, 512 - MLA_Q_RANK), (0, 0))).astype(BF16)


def _rope_tables():
    t = np.arange(LAT_S)
    rows, cols = (t // GRID_W).astype(np.float32), (t % GRID_W).astype(np.float32)

    def ang(d):
        n = d // 4
        freqs = jnp.asarray(ROPE_THETA, F32) ** (-jnp.arange(n, dtype=F32) / n)
        return jnp.concatenate([rows[:, None] * freqs, cols[:, None] * freqs], axis=-1)

    a = ang(MLA_ROPE)
    c, s, z = jnp.cos(a), jnp.sin(a), jnp.zeros_like(a)
    tab64 = (jnp.concatenate([c, c, z, z], -1), jnp.concatenate([-s, z, z, z], -1),
             jnp.concatenate([z, s, z, z], -1))
    a = ang(HD)
    c, s, z = jnp.cos(a), jnp.sin(a), jnp.zeros_like(a)
    tab128 = (jnp.concatenate([c, c], -1), jnp.concatenate([-s, z], -1), jnp.concatenate([z, s], -1))
    return tab64, tab128


def kernel(x_prompt, x_sample, c, cache_mla_ckv, cache_mla_krope, cache_na_k, cache_na_v, cache_gqa_k,
           cache_gqa_v, cache_swa_k, cache_swa_v, c_ctx, w_mod, b_mod, norm1_g, w_in, mla_q_norm_g, mla_w_uq,
           mla_kv_norm_g, mla_w_uk, mla_w_uv, na_rpb, gqa_q_norm_g, gqa_k_norm_g, swa_sink, w_out, norm2_g,
           peer_w_q, peer_subkeys, peer_u, peer_v, final_norm_g):
    x = jnp.concatenate([x_prompt.reshape(N_CTX, D), x_sample.reshape(N_LAT_B * LAT_S, D)], axis=0)
    cond8 = jnp.concatenate([c_ctx[None, :], c, jnp.zeros((3, D), F32)], axis=0)
    mod_all = _modulation(cond8, w_mod, b_mod).reshape(DEPTH, 8, N_MOD, D)
    tab64, tab128 = _rope_tables()
    ckr_p = jnp.pad(cache_mla_krope, ((0, 0), (0, 0), (0, 0), (0, HD - MLA_ROPE)))
    c_na_k = cache_na_k.reshape(N_LAT_B, DEPTH, PAST, 512)
    c_na_v = cache_na_v.reshape(N_LAT_B, DEPTH, PAST, 512)
    c_gqa_k = cache_gqa_k.reshape(N_LAT_B, DEPTH, PAST, 256)
    c_gqa_v = cache_gqa_v.reshape(N_LAT_B, DEPTH, PAST, 256)
    c_swa_k = cache_swa_k.reshape(N_LAT_B, DEPTH, PAST, 256)
    c_swa_v = cache_swa_v.reshape(N_LAT_B, DEPTH, PAST, 256)

    states = []
    for l in range(DEPTH):
        mod = mod_all[l]
        proj = _inproj(x, mod, norm1_g[l][None, :], _prep_w_in(w_in[l]))
        mla_w = (jnp.pad(mla_q_norm_g[l], (0, 512 - MLA_Q_RANK))[None, :], _prep_w_uq(mla_w_uq[l]),
                 mla_kv_norm_g[l][None, :], mla_w_uk[l].astype(BF16), mla_w_uv[l].astype(BF16))
        gqa_w = (gqa_q_norm_g[l][None, :], gqa_k_norm_g[l][None, :])
        oa_c, st_ckv, st_kr = _mla(proj, mla_w, l, False)
        oa_l, = _mla(proj, mla_w, l, True, (cache_mla_ckv, ckr_p), tab64)
        ob_c, st_nk, st_nv = _na(proj, l, False)
        ob_l, = _na(proj, l, True, (c_na_k, c_na_v), _na_bias(na_rpb[l]))
        oc_c, st_gk, st_gv = _gqa(proj, gqa_w, l, False)
        oc_l, = _gqa(proj, gqa_w, l, True, (c_gqa_k, c_gqa_v), tab128)
        od_c, st_sk, st_sv = _swa(proj, swa_sink[l], l, False)
        od_l, = _swa(proj, swa_sink[l], l, True, (c_swa_k, c_swa_v), tab128)
        os_ = [jnp.concatenate([a, b], axis=0) for a, b in ((oa_c, oa_l), (ob_c, ob_l), (oc_c, oc_l), (od_c, od_l))]
        x1, h2t = _outproj(os_, x, mod, norm2_g[l][None, :], w_out[l].astype(BF16).reshape(4, 512, D))
        s1, s2, e1, e2, thr = _peer_scores(h2t, peer_w_q[l].T.astype(BF16),
                                           peer_subkeys[l].reshape(2 * PEER_HEADS, N_KEYS, N_KEYS).astype(BF16))
        key_major = lambda t: t.transpose(2, 0, 1, 3).reshape(N_KEYS, PEER_HEADS, N_TOK)
        stats = (key_major(s1), s2, key_major(e1), e2, thr)
        x = _peer_main(h2t, peer_u[l].astype(BF16), peer_v[l].T.astype(BF16), stats, x1, mod)
        states.append((st_ckv, st_kr, st_nk, st_nv, st_gk, st_gv, st_sk, st_sv))

    g = final_norm_g[None, :]
    y_prompt = _final_norm(x, g, 0, N_CTX).reshape(N_CTX_B, CTX_S, D)
    y_sample = _final_norm(x, g, N_CTX // 512, N_LAT_B * LAT_S).reshape(N_LAT_B, LAT_S, D)
    tails = ((MLA_KV_RANK,), (MLA_ROPE,), (4, HD), (4, HD), (2, HD), (2, HD), (2, HD), (2, HD))
    outs = [jnp.stack([st[k].reshape((N_CTX_B, CTX_S) + tails[k]) for st in states], axis=1) for k in range(8)]
    return (y_prompt, y_sample, *outs)
```

```python
import functools

import numpy as np
import jax
import jax.numpy as jnp
from jax import lax
from jax.experimental import pallas as pl
from jax.experimental.pallas import tpu as pltpu

F32 = jnp.float32
BF16 = jnp.bfloat16

D = 2048
N_CTX_B, CTX_S = 16, 256
N_LAT_B, LAT_S = 4, 1024
N_CTX = N_CTX_B * CTX_S
N_TOK = N_CTX + N_LAT_B * LAT_S
DEPTH = 2
PAST = 512
GRID_W = 64
HD = 128
MLA_Q_RANK, MLA_KV_RANK, MLA_ROPE = 384, 256, 64
NA_ROWS, NA_COLS = 8, 16
SWA_W = 128
N_KEYS = 128
PEER_HEADS = 8
PEER_TOPK = 16
N_MOD = 6
EPS = 1e-6
NEG_INF = -1e30
ROPE_THETA = 10000.0
LANE = 128

IN_SPLITS = (384, 256, 64, 512, 512, 512, 512, 256, 256, 512, 256, 256)
D_IN_P = 4608
COL = dict(na_q=0, na_k=4, na_v=8, c_q=12, d_q=16, q_lat=20, ckv=24, c_k=26, c_v=28, d_k=30, d_v=32,
           krope=34)

VMEM_LIMIT = 56 * 1024 * 1024


def _params(sem):
    return pltpu.CompilerParams(dimension_semantics=sem, vmem_limit_bytes=VMEM_LIMIT)


def _full(shape):
    nd = len(shape)
    return pl.BlockSpec(shape, lambda *_: (0,) * nd)


def _dot(a, b):
    return jnp.dot(a, b, preferred_element_type=F32)


def _dot_nt(a, b):
    return lax.dot_general(a, b, (((1,), (1,)), ((), ())), preferred_element_type=F32)


def _cond_of_rows(row0):
    return jnp.where(row0 < N_CTX, 0, 1 + (row0 - N_CTX) // LAT_S)


def _mod_kernel(c_ref, w_ref, b_ref, o_ref):
    c = c_ref[...]
    s = c * jax.nn.sigmoid(c)
    o_ref[0] = _dot(s.astype(BF16), w_ref[0].astype(BF16)) + b_ref[0]


def _modulation(cond8, w_mod, b_mod):
    tn = 1536
    n = N_MOD * D
    return pl.pallas_call(
        _mod_kernel,
        grid=(DEPTH, n // tn),
        in_specs=[pl.BlockSpec((8, D), lambda l, j: (0, 0)),
                  pl.BlockSpec((1, D, tn), lambda l, j: (l, 0, j)),
                  pl.BlockSpec((1, 1, tn), lambda l, j: (l, 0, j))],
        out_specs=pl.BlockSpec((1, 8, tn), lambda l, j: (l, 0, j)),
        out_shape=jax.ShapeDtypeStruct((DEPTH, 8, n), F32),
        compiler_params=_params(("arbitrary", "arbitrary")),
        name="modulation",
    )(cond8, w_mod, b_mod.reshape(DEPTH, 1, n))


def _inproj_kernel(x_ref, mod_ref, g_ref, w_ref, o_ref):
    x = x_ref[...]
    y = x * lax.rsqrt(jnp.mean(x * x, axis=-1, keepdims=True) + EPS) * g_ref[...]
    m = mod_ref[0]
    h = y * (1.0 + m[1:2, :]) + m[0:1, :]
    o_ref[...] = _dot(h.astype(BF16), w_ref[...])


def _inproj(x, mod, g, w_p):
    tm, tn = 512, 2304
    return pl.pallas_call(
        _inproj_kernel,
        grid=(D_IN_P // tn, N_TOK // tm),
        in_specs=[pl.BlockSpec((tm, D), lambda j, i: (i, 0)),
                  pl.BlockSpec((1, N_MOD, D), lambda j, i: (_cond_of_rows(i * tm), 0, 0)),
                  pl.BlockSpec((1, D), lambda j, i: (0, 0)),
                  pl.BlockSpec((D, tn), lambda j, i: (0, j))],
        out_specs=pl.BlockSpec((tm, tn), lambda j, i: (i, j)),
        out_shape=jax.ShapeDtypeStruct((N_TOK, D_IN_P), F32),
        compiler_params=_params(("arbitrary", "arbitrary")),
        name="inproj",
    )(x, mod, g, w_p)


def _rms(x, g, n=None):
    n = x.shape[-1] if n is None else n
    return x * lax.rsqrt(jnp.sum(x * x, axis=-1, keepdims=True) * (1.0 / n) + EPS) * g


def _rope(x, cos, sin_lo, sin_hi, shift):
    return (x * cos + pltpu.roll(x, LANE - shift, 1) * sin_lo + pltpu.roll(x, shift, 1) * sin_hi)


def _softmax_pv(parts, sink=None):
    m = parts[0][0].max(axis=-1, keepdims=True)
    for s, _ in parts[1:]:
        m = jnp.maximum(m, s.max(axis=-1, keepdims=True))
    if sink is not None:
        m = jnp.maximum(m, sink)
    es = [jnp.exp(s - m) for s, _ in parts]
    den = es[0].sum(axis=-1, keepdims=True)
    for e in es[1:]:
        den = den + e.sum(axis=-1, keepdims=True)
    if sink is not None:
        den = den + jnp.exp(sink - m)
    inv = 1.0 / den
    o = None
    for e, (_, v) in zip(es, parts):
        t = _dot((e * inv).astype(BF16), v)
        o = t if o is None else o + t
    return o


QB = 256


def _attn_call(kern, name, latent, l, in_specs, args, state_widths, carried, scratch=()):
    s_len = LAT_S if latent else CTX_S
    nb = N_LAT_B if latent else N_CTX_B
    rb = (N_CTX // s_len) if latent else 0
    out_specs = [pl.BlockSpec((s_len, 512), lambda b: (rb + b, 0))]
    out_shape = [jax.ShapeDtypeStruct((N_TOK, 512), BF16)]
    if not latent:
        out_specs += [pl.BlockSpec((1, 1, s_len, w), lambda b: (b, l, 0, 0)) for w in state_widths]
        out_shape += [jax.ShapeDtypeStruct((nb, DEPTH, s_len, w), F32) for w in state_widths]
    n_in = len(args)
    carried = [c for c in carried if c is not None]
    first_out = len(out_shape) - len(carried)

    def body(*refs):
        return kern(*refs[:n_in], *refs[n_in + len(carried):])

    return pl.pallas_call(
        body, grid=(nb,),
        in_specs=list(in_specs) + [pl.BlockSpec(memory_space=pl.ANY)] * len(carried),
        out_specs=out_specs, out_shape=out_shape, scratch_shapes=list(scratch),
        input_output_aliases={n_in + i: first_out + i for i in range(len(carried))},
        compiler_params=_params(("arbitrary",)), name=name,
    )(*args, *carried)


def _mla_kernel(*refs, latent, s_len):
    if latent:
        (ql_ref, ckv_ref, kr_ref, cckv_ref, ckr_ref, cos_ref, sa_ref, sb_ref,
         qg_ref, wuq_ref, kvg_ref, wuk_ref, wuv_ref, o_ref, ckv_scr, q_scr, k_scr, v_scr) = refs
    else:
        (ql_ref, ckv_ref, kr_ref, qg_ref, wuq_ref, kvg_ref, wuk_ref, wuv_ref,
         o_ref, ckv_out, kr_out, ckv_scr, q_scr, k_scr, v_scr) = refs
    p_len = PAST if latent else 0
    qn = _rms(ql_ref[...], qg_ref[...], MLA_Q_RANK).astype(BF16)
    ckv_n = _rms(ckv_ref[...], kvg_ref[...])
    kr = kr_ref[...]
    if latent:
        cos, sa, sb = cos_ref[...], sa_ref[...], sb_ref[...]
        kr = _rope(kr, cos, sa, sb, 32)
        ckv_scr[0:p_len, :] = cckv_ref[0, 0].astype(BF16)
        kr_all = jnp.concatenate([ckr_ref[0, 0], kr], axis=0)
    else:
        ckv_out[0, 0] = ckv_n
        kr_out[0, 0] = kr[:, :MLA_ROPE]
        kr_all = kr
    ckv_scr[p_len:, :] = ckv_n.astype(BF16)
    kr_all = kr_all.astype(BF16)
    ckv_all = ckv_scr[...]
    for h in range(4):
        q_scr[h, :, 0:HD] = _dot(qn, wuq_ref[:, h * HD:(h + 1) * HD]).astype(BF16)
        qr = _dot(qn, wuq_ref[:, (4 + h) * HD:(5 + h) * HD])
        if latent:
            qr = _rope(qr, cos, sa, sb, 32)
        q_scr[h, :, HD:2 * HD] = qr.astype(BF16)
        k_scr[h, :, 0:HD] = _dot(ckv_all, wuk_ref[:, h * HD:(h + 1) * HD]).astype(BF16)
        k_scr[h, :, HD:2 * HD] = kr_all
        v_scr[h] = _dot(ckv_all, wuv_ref[:, h * HD:(h + 1) * HD]).astype(BF16)
    scale = float(HD + MLA_ROPE) ** -0.5
    for h in range(4):
        for qb in range(s_len // QB):
            s = _dot_nt(q_scr[h, qb * QB:(qb + 1) * QB, :], k_scr[h]) * scale
            o = _softmax_pv([(s, v_scr[h])])
            o_ref[qb * QB:(qb + 1) * QB, h * HD:(h + 1) * HD] = o.astype(BF16)


def _mla(proj, wts, l, latent, carried, caches=None, tabs=None):
    s_len = LAT_S if latent else CTX_S
    rb = (N_CTX // s_len) if latent else 0
    t_len = s_len + (PAST if latent else 0)
    qg, wuq, kvg, wuk, wuv = wts
    in_specs = [pl.BlockSpec((s_len, 512), lambda b: (rb + b, COL["q_lat"] // 4)),
                pl.BlockSpec((s_len, 256), lambda b: (rb + b, COL["ckv"] // 2)),
                pl.BlockSpec((s_len, 128), lambda b: (rb + b, COL["krope"]))]
    args = [proj, proj, proj]
    if latent:
        in_specs += [pl.BlockSpec((1, 1, PAST, 256), lambda b: (b, l, 0, 0)),
                     pl.BlockSpec((1, 1, PAST, 128), lambda b: (b, l, 0, 0)),
                     _full((s_len, 128)), _full((s_len, 128)), _full((s_len, 128))]
        args += [caches[0], caches[1], *tabs]
    in_specs += [_full((1, 512)), _full((512, 1024)), _full((1, 256)), _full((256, 512)), _full((256, 512))]
    args += [qg, wuq, kvg, wuk, wuv]
    return _attn_call(
        functools.partial(_mla_kernel, latent=latent, s_len=s_len), "mla_latent" if latent else "mla_context",
        latent, l, in_specs, args, (MLA_KV_RANK, MLA_ROPE), carried,
        scratch=[pltpu.VMEM((t_len, 256), BF16), pltpu.VMEM((4, s_len, 256), BF16),
                 pltpu.VMEM((4, t_len, 256), BF16), pltpu.VMEM((4, t_len, HD), BF16)])


def _na_kernel(*refs, latent, s_len):
    if latent:
        q_ref, k_ref, v_ref, ck_ref, cv_ref, bias_ref, o_ref = refs
    else:
        q_ref, k_ref, v_ref, o_ref, k_out, v_out = refs
        k_out[0, 0] = k_ref[...]
        v_out[0, 0] = v_ref[...]
    scale = float(HD) ** -0.5
    rows = s_len // GRID_W
    for h in range(4):
        cs = slice(h * HD, (h + 1) * HD)
        if not latent:
            s = _dot_nt(q_ref[:, cs].astype(BF16), k_ref[:, cs].astype(BF16)) * scale
            o_ref[:, cs] = _softmax_pv([(s, v_ref[:, cs].astype(BF16))]).astype(BF16)
            continue
        kc = ck_ref[0, 0, :, cs].astype(BF16)
        vc = cv_ref[0, 0, :, cs].astype(BF16)
        for r in range(rows):
            kr0 = min(max(r - NA_ROWS // 2, 0), rows - NA_ROWS)
            win = slice(kr0 * GRID_W, (kr0 + NA_ROWS) * GRID_W)
            q = q_ref[r * GRID_W:(r + 1) * GRID_W, cs].astype(BF16)
            s_nb = _dot_nt(q, k_ref[win, cs].astype(BF16)) * scale + bias_ref[h, kr0 - r + NA_ROWS - 1]
            s_c = _dot_nt(q, kc) * scale
            o = _softmax_pv([(s_nb, v_ref[win, cs].astype(BF16)), (s_c, vc)])
            o_ref[r * GRID_W:(r + 1) * GRID_W, cs] = o.astype(BF16)


def _na(proj, l, latent, carried, caches=None, bias=None):
    s_len = LAT_S if latent else CTX_S
    rb = (N_CTX // s_len) if latent else 0
    in_specs = [pl.BlockSpec((s_len, 512), lambda b: (rb + b, COL["na_q"] // 4)),
                pl.BlockSpec((s_len, 512), lambda b: (rb + b, COL["na_k"] // 4)),
                pl.BlockSpec((s_len, 512), lambda b: (rb + b, COL["na_v"] // 4))]
    args = [proj, proj, proj]
    if latent:
        in_specs += [pl.BlockSpec((1, 1, PAST, 512), lambda b: (b, l, 0, 0)),
                     pl.BlockSpec((1, 1, PAST, 512), lambda b: (b, l, 0, 0)),
                     _full(bias.shape)]
        args += [caches[0], caches[1], bias]
    return _attn_call(functools.partial(_na_kernel, latent=latent, s_len=s_len),
                      "na_latent" if latent else "na_context", latent, l, in_specs, args, (512, 512), carried)


def _na_bias(rpb):
    w = GRID_W
    side = w - NA_COLS
    p = jnp.pad(rpb, ((0, 0), (0, 0), (side, side + 1)))
    t = jnp.tile(p, (1, 1, w))[..., :w * (2 * w - 1)].reshape(4, 2 * NA_ROWS - 1, w, 2 * w - 1)[..., w - 1:]
    qc = np.arange(w)
    cs = np.clip(qc - NA_COLS // 2, 0, w - NA_COLS)
    valid = (qc[None, :] >= cs[:, None]) & (qc[None, :] < cs[:, None] + NA_COLS)
    t = jnp.where(valid[None, None], t, NEG_INF)
    return jnp.stack([jnp.concatenate([t[:, d0 + j] for j in range(NA_ROWS)], axis=-1)
                      for d0 in range(NA_ROWS)], axis=1).astype(F32)


def _gqa_kernel(*refs, latent, s_len):
    if latent:
        q_ref, k_ref, v_ref, ck_ref, cv_ref, cos_ref, sl_ref, sh_ref, qg_ref, kg_ref, o_ref, k_scr, v_scr = refs
        cos, sl, sh = cos_ref[...], sl_ref[...], sh_ref[...]
    else:
        q_ref, k_ref, v_ref, qg_ref, kg_ref, o_ref, k_out, v_out, k_scr, v_scr = refs
        v_out[0, 0] = v_ref[...]
    p_len = PAST if latent else 0
    scale = float(HD) ** -0.5
    for g in range(2):
        cs = slice(g * HD, (g + 1) * HD)
        kn = _rms(k_ref[:, cs], kg_ref[...])
        if latent:
            kn = _rope(kn, cos, sl, sh, 64)
            k_scr[g, 0:p_len, :] = ck_ref[0, 0, :, cs].astype(BF16)
            v_scr[g, 0:p_len, :] = cv_ref[0, 0, :, cs].astype(BF16)
        else:
            k_out[0, 0, :, cs] = kn
        k_scr[g, p_len:, :] = kn.astype(BF16)
        v_scr[g, p_len:, :] = v_ref[:, cs].astype(BF16)
    for hq in range(4):
        g = hq // 2
        cs = slice(hq * HD, (hq + 1) * HD)
        for qb in range(s_len // QB):
            rs = slice(qb * QB, (qb + 1) * QB)
            q = _rms(q_ref[rs, cs], qg_ref[...])
            if latent:
                q = _rope(q, cos[rs], sl[rs], sh[rs], 64)
            s = _dot_nt(q.astype(BF16), k_scr[g]) * scale
            o_ref[rs, cs] = _softmax_pv([(s, v_scr[g])]).astype(BF16)


def _gqa(proj, wts, l, latent, carried, caches=None, tabs=None):
    s_len = LAT_S if latent else CTX_S
    rb = (N_CTX // s_len) if latent else 0
    t_len = s_len + (PAST if latent else 0)
    in_specs = [pl.BlockSpec((s_len, 512), lambda b: (rb + b, COL["c_q"] // 4)),
                pl.BlockSpec((s_len, 256), lambda b: (rb + b, COL["c_k"] // 2)),
                pl.BlockSpec((s_len, 256), lambda b: (rb + b, COL["c_v"] // 2))]
    args = [proj, proj, proj]
    if latent:
        in_specs += [pl.BlockSpec((1, 1, PAST, 256), lambda b: (b, l, 0, 0)),
                     pl.BlockSpec((1, 1, PAST, 256), lambda b: (b, l, 0, 0)),
                     _full((s_len, 128)), _full((s_len, 128)), _full((s_len, 128))]
        args += [caches[0], caches[1], *tabs]
    in_specs += [_full((1, HD)), _full((1, HD))]
    args += list(wts)
    return _attn_call(
        functools.partial(_gqa_kernel, latent=latent, s_len=s_len), "gqa_latent" if latent else "gqa_context",
        latent, l, in_specs, args, (256, 256), carried,
        scratch=[pltpu.VMEM((2, t_len, HD), BF16), pltpu.VMEM((2, t_len, HD), BF16)])


def _swa_kernel(*refs, latent, s_len):
    if latent:
        q_ref, k_ref, v_ref, ck_ref, cv_ref, cos_ref, sl_ref, sh_ref, sink_ref, o_ref, k_scr, v_scr = refs
        cos, sl, sh = cos_ref[...], sl_ref[...], sh_ref[...]
    else:
        q_ref, k_ref, v_ref, sink_ref, o_ref, k_out, v_out, k_scr, v_scr = refs
        k_out[0, 0] = k_ref[...]
        v_out[0, 0] = v_ref[...]
    scale = float(HD) ** -0.5
    for g in range(2):
        cs = slice(g * HD, (g + 1) * HD)
        k = k_ref[:, cs]
        if latent:
            k = _rope(k, cos, sl, sh, 64)
        k_scr[g] = k.astype(BF16)
        v_scr[g] = v_ref[:, cs].astype(BF16)
    for hq in range(4):
        g = hq // 2
        cs = slice(hq * HD, (hq + 1) * HD)
        gs = slice(g * HD, (g + 1) * HD)
        sink = sink_ref[hq]
        if not latent:
            s = _dot_nt(q_ref[:, cs].astype(BF16), k_scr[g]) * scale
            o_ref[:, cs] = _softmax_pv([(s, v_scr[g])], sink=sink).astype(BF16)
            continue
        kc = ck_ref[0, 0, :, gs].astype(BF16)
        vc = cv_ref[0, 0, :, gs].astype(BF16)
        for n in range(s_len // SWA_W):
            rs = slice(n * SWA_W, (n + 1) * SWA_W)
            lo, hi = max(0, (n - 1) * SWA_W), min(s_len, (n + 2) * SWA_W)
            q = _rope(q_ref[rs, cs], cos[rs], sl[rs], sh[rs], 64).astype(BF16)
            s_w = _dot_nt(q, k_scr[g, lo:hi, :]) * scale
            qpos = n * SWA_W + lax.broadcasted_iota(jnp.int32, s_w.shape, 0)
            kpos = lo + lax.broadcasted_iota(jnp.int32, s_w.shape, 1)
            s_w = jnp.where(jnp.abs(kpos - qpos) <= SWA_W, s_w, NEG_INF)
            s_c = _dot_nt(q, kc) * scale
            o = _softmax_pv([(s_w, v_scr[g, lo:hi, :]), (s_c, vc)], sink=sink)
            o_ref[rs, cs] = o.astype(BF16)


def _swa(proj, sink, l, latent, carried, caches=None, tabs=None):
    s_len = LAT_S if latent else CTX_S
    rb = (N_CTX // s_len) if latent else 0
    in_specs = [pl.BlockSpec((s_len, 512), lambda b: (rb + b, COL["d_q"] // 4)),
                pl.BlockSpec((s_len, 256), lambda b: (rb + b, COL["d_k"] // 2)),
                pl.BlockSpec((s_len, 256), lambda b: (rb + b, COL["d_v"] // 2))]
    args = [proj, proj, proj]
    if latent:
        in_specs += [pl.BlockSpec((1, 1, PAST, 256), lambda b: (b, l, 0, 0)),
                     pl.BlockSpec((1, 1, PAST, 256), lambda b: (b, l, 0, 0)),
                     _full((s_len, 128)), _full((s_len, 128)), _full((s_len, 128))]
        args += [caches[0], caches[1], *tabs]
    in_specs += [pl.BlockSpec(memory_space=pltpu.SMEM)]
    args += [sink]
    return _attn_call(
        functools.partial(_swa_kernel, latent=latent, s_len=s_len), "swa_latent" if latent else "swa_context",
        latent, l, in_specs, args, (256, 256), carried,
        scratch=[pltpu.VMEM((2, s_len, HD), BF16), pltpu.VMEM((2, s_len, HD), BF16)])


def _outproj_kernel(oa_ref, ob_ref, oc_ref, od_ref, x_ref, mod_ref, g_ref, w_ref, x1_ref, h2t_ref):
    acc = _dot(oa_ref[...], w_ref[0])
    acc = acc + _dot(ob_ref[...], w_ref[1])
    acc = acc + _dot(oc_ref[...], w_ref[2])
    acc = acc + _dot(od_ref[...], w_ref[3])
    m = mod_ref[0]
    x1 = x_ref[...] + m[2:3, :] * acc
    x1_ref[...] = x1
    y = x1 * lax.rsqrt(jnp.mean(x1 * x1, axis=-1, keepdims=True) + EPS) * g_ref[...]
    h2 = y * (1.0 + m[4:5, :]) + m[3:4, :]
    for i in range(h2t_ref.shape[0]):
        h2t_ref[i] = h2[i * LANE:(i + 1) * LANE, :].T.astype(BF16)


def _outproj(os_, x, mod, g, w4):
    tm = 512
    row = lambda i: (i, 0)
    return pl.pallas_call(
        _outproj_kernel,
        grid=(N_TOK // tm,),
        in_specs=[pl.BlockSpec((tm, 512), row)] * 4 + [
            pl.BlockSpec((tm, D), row),
            pl.BlockSpec((1, N_MOD, D), lambda i: (_cond_of_rows(i * tm), 0, 0)),
            _full((1, D)), _full((4, 512, D))],
        out_specs=[pl.BlockSpec((tm, D), row), pl.BlockSpec((tm // LANE, D, LANE), lambda i: (i, 0, 0))],
        out_shape=[jax.ShapeDtypeStruct((N_TOK, D), F32), jax.ShapeDtypeStruct((N_TOK // LANE, D, LANE), BF16)],
        compiler_params=_params(("arbitrary",)),
        name="outproj",
    )(*os_, x, mod, g, w4)


PEER_TB = 512


def _lane_tiles(ref):
    return jnp.concatenate([ref[i] for i in range(ref.shape[0])], axis=1)


def _cmp_exchange(vals, i, j):
    a, b = vals[i], vals[j]
    if b is None:
        return
    if a is None:
        vals[i], vals[j] = b, None
    else:
        vals[i], vals[j] = jnp.maximum(a, b), jnp.minimum(a, b)


def _bitonic_merge16(vals):
    for j in (8, 4, 2, 1):
        for i in range(16):
            if i ^ j > i:
                _cmp_exchange(vals, i, i ^ j)
    return vals


def _sort16_desc(vals):
    vals = list(vals)
    for k in (2, 4, 8, 16):
        j = k // 2
        while j >= 1:
            for i in range(16):
                l = i ^ j
                if l > i:
                    _cmp_exchange(vals, *((i, l) if (i & k) == 0 else (l, i)))
            j //= 2
    return vals


def _merge_top16(a, b):
    c = []
    for i in range(16):
        x, y = a[i], b[15 - i]
        c.append(y if x is None else x if y is None else jnp.maximum(x, y))
    return _bitonic_merge16(c)


def _top16_of_keys(x):
    vals = _sort16_desc([x[v * 8:(v + 1) * 8] for v in range(N_KEYS // 8)])
    for shift in (4, 2, 1):
        vals = _merge_top16(vals, [pltpu.roll(v, shift, 0) for v in vals])
    return vals


def _top16_pair_sums(d1, d2):
    pad = lambda lst: lst + [None] * (16 - len(lst))
    cur = pad([d1[0] + d2[b] for b in range(16)])
    for a in range(1, 8):
        cur = _merge_top16(cur, pad([d1[a] + d2[b] for b in range(16 // (a + 1))]))
    return _merge_top16(cur, pad([d1[a] + d2[0] for a in range(8, 16)]))


def _peer_score_kernel(h2t_ref, wq_ref, sk_ref, s1_ref, s2_ref, e1_ref, e2_ref, thr_ref, q_scr):
    n_lt = PEER_TB // LANE
    q_scr[...] = _dot(wq_ref[...], _lane_tiles(h2t_ref)).astype(BF16)
    for h in range(PEER_HEADS):
        for p, dst in ((0, s1_ref), (1, s2_ref)):
            hp = 2 * h + p
            s = _dot(sk_ref[hp], q_scr[hp * N_KEYS:(hp + 1) * N_KEYS, :])
            for lt in range(n_lt):
                dst[h, lt] = s[:, lt * LANE:(lt + 1) * LANE]

    def tile(i, carry):
        h = i // n_lt
        lt = i % n_lt
        s1 = s1_ref[h, lt]
        s2 = s2_ref[h, lt]
        d1 = _top16_of_keys(s1)
        d2 = _top16_of_keys(s2)
        best = _top16_pair_sums(d1, d2)
        z = jnp.ones_like(best[0])
        for c in best[1:]:
            z = z + jnp.exp(c - best[0])
        zinv = 1.0 / z
        thr_ref[h, lt] = best[15][0:1, :]
        e1_ref[h, lt] = jnp.exp(s1 - d1[0][0:1, :]) * (0.5 * zinv[0:1, :])
        e2_ref[h, lt] = jnp.exp(s2 - d2[0][0:1, :])
        return carry

    lax.fori_loop(0, PEER_HEADS * n_lt, tile, 0)


def _peer_scores(h2t, wq_t, subkeys):
    tb = PEER_TB
    n_lt = tb // LANE
    tiles = pl.BlockSpec((PEER_HEADS, n_lt, N_KEYS, LANE), lambda i: (0, i, 0, 0))
    sds = jax.ShapeDtypeStruct((PEER_HEADS, N_TOK // LANE, N_KEYS, LANE), F32)
    return pl.pallas_call(
        _peer_score_kernel,
        grid=(N_TOK // tb,),
        in_specs=[pl.BlockSpec((n_lt, D, LANE), lambda i: (i, 0, 0)), _full((D, D)),
                  _full((2 * PEER_HEADS, N_KEYS, N_KEYS))],
        out_specs=[tiles, tiles, tiles, tiles, pl.BlockSpec((PEER_HEADS, n_lt, 1, LANE), lambda i: (0, i, 0, 0))],
        out_shape=[sds, sds, sds, sds, jax.ShapeDtypeStruct((PEER_HEADS, N_TOK // LANE, 1, LANE), F32)],
        scratch_shapes=[pltpu.VMEM((D, tb), BF16)],
        compiler_params=_params(("arbitrary",)),
        name="peer_scores",
    )(h2t, wq_t, subkeys)


PEER_EB = 1024


N_CHUNK = N_KEYS * N_KEYS // PEER_EB
N_BLK = N_TOK // PEER_TB
N_WORK = N_BLK * N_CHUNK
I1_PER_CHUNK = PEER_EB // N_KEYS
A_PER_IT = 4
KEY_ROWS = 32
V_ROWS = D * A_PER_IT // I1_PER_CHUNK


def _peer_main_kernel(h2t_ref, u_ref, vt_ref, s1_ref, s2_ref, e1_ref, e2_ref, thr_ref, x1_ref, mod_ref,
                      x2_ref, acc_ref, act0, act1, wt0, wt1):
    s = pl.program_id(0)
    chunk_c = jnp.clip(s - 2, 0, N_WORK - 1) % N_CHUNK

    @pl.when(s == 0)
    def _():
        for ref in (act0, act1, wt0, wt1, acc_ref):
            ref[...] = jnp.zeros_like(ref)

    def stages(act_new, act_cur, wt_cur, wt_old):
        def body(it, carry):
            a0 = it * A_PER_IT

            def gate_tile(lt, kq):
                lanes = slice(lt * LANE, (lt + 1) * LANE)
                keys = slice(kq * KEY_ROWS, (kq + 1) * KEY_ROWS)
                g = [jnp.zeros((KEY_ROWS, LANE), F32) for _ in range(A_PER_IT)]
                for h in range(PEER_HEADS):
                    s2 = s2_ref[h, lt, keys, :]
                    e2 = e2_ref[h, lt, keys, :]
                    thr = thr_ref[h, lt]
                    for k in range(A_PER_IT):
                        tot = s2 + s1_ref[a0 + k, h:h + 1, lanes]
                        gate = e2 * e1_ref[a0 + k, h:h + 1, lanes]
                        g[k] = g[k] + jnp.where(tot >= thr, gate, 0.0)
                for k in range(A_PER_IT):
                    r = pl.ds(pl.multiple_of((a0 + k) * N_KEYS + kq * KEY_ROWS, KEY_ROWS), KEY_ROWS)
                    act = act_cur[lt, r, :]
                    w = g[k] * (act * (1.0 + lax.erf(act * (2.0 ** -0.5))))
                    wt_cur[lt, r, :] = w.astype(BF16)

            def project(m, n):
                rows = pl.ds(pl.multiple_of((a0 + m) * N_KEYS, N_KEYS), N_KEYS)
                rhs = jnp.concatenate([h2t_ref[2 * n], h2t_ref[2 * n + 1]], axis=1)
                res = _dot(u_ref[rows, :], rhs)
                act_new[2 * n, rows, :] = res[:, :LANE]
                act_new[2 * n + 1, rows, :] = res[:, LANE:]

            def fold(m, n):
                vr = V_ROWS // A_PER_IT
                vrows = pl.ds(pl.multiple_of(it * V_ROWS + m * vr, vr), vr)
                cols = slice(2 * n * LANE, (2 * n + 2) * LANE)
                rhs = jnp.concatenate([wt_old[2 * n], wt_old[2 * n + 1]], axis=1)
                acc_ref[vrows, cols] += _dot(vt_ref[vrows, :], rhs)

            mxu = [functools.partial(f, m, n) for f in (project, fold) for m in range(A_PER_IT) for n in range(2)]
            n_kq = N_KEYS // KEY_ROWS
            for i in range(len(mxu)):
                mxu[i]()
                gate_tile(i // n_kq, i % n_kq)
            return carry
        return body

    @pl.when(s % 2 == 0)
    def _():
        lax.fori_loop(0, I1_PER_CHUNK // A_PER_IT, stages(act0, act1, wt1, wt0), 0)

    @pl.when(s % 2 == 1)
    def _():
        lax.fori_loop(0, I1_PER_CHUNK // A_PER_IT, stages(act1, act0, wt0, wt1), 0)

    @pl.when(chunk_c == N_CHUNK - 1)
    def _():
        x2_ref[...] = x1_ref[...] + mod_ref[0][5:6, :] * acc_ref[...].T
        acc_ref[...] = jnp.zeros_like(acc_ref)


def _peer_main(h2t, u_bf, vt_bf, stats, x1, mod):
    tb, eb = PEER_TB, PEER_EB
    s1x, s2, e1x, e2, thr = stats
    item_a = lambda s: jnp.minimum(s, N_WORK - 1)
    item_b = lambda s: jnp.clip(s - 1, 0, N_WORK - 1)
    item_c = lambda s: jnp.clip(s - 2, 0, N_WORK - 1)
    once = pl.Buffered(1)
    n_lt = tb // LANE
    tiles = pl.BlockSpec((PEER_HEADS, n_lt, N_KEYS, LANE), lambda s: (0, item_b(s) // N_CHUNK, 0, 0),
                         pipeline_mode=once)
    row8 = pl.BlockSpec((I1_PER_CHUNK, PEER_HEADS, tb), lambda s: (item_b(s) % N_CHUNK, 0, item_b(s) // N_CHUNK))
    return pl.pallas_call(
        _peer_main_kernel,
        grid=(N_WORK + 2,),
        in_specs=[pl.BlockSpec((n_lt, D, LANE), lambda s: (item_a(s) // N_CHUNK, 0, 0)),
                  pl.BlockSpec((eb, D), lambda s: (item_a(s) % N_CHUNK, 0)),
                  pl.BlockSpec((D, eb), lambda s: (0, item_c(s) % N_CHUNK)),
                  row8, tiles, row8, tiles,
                  pl.BlockSpec((PEER_HEADS, n_lt, 1, LANE), lambda s: (0, item_b(s) // N_CHUNK, 0, 0)),
                  pl.BlockSpec((tb, D), lambda s: (item_c(s) // N_CHUNK, 0), pipeline_mode=once),
                  pl.BlockSpec((1, N_MOD, D), lambda s: (_cond_of_rows((item_c(s) // N_CHUNK) * tb), 0, 0))],
        out_specs=pl.BlockSpec((tb, D), lambda s: (item_c(s) // N_CHUNK, 0)),
        out_shape=jax.ShapeDtypeStruct((N_TOK, D), F32),
        scratch_shapes=[pltpu.VMEM((D, tb), F32),
                        pltpu.VMEM((n_lt, eb, LANE), F32), pltpu.VMEM((n_lt, eb, LANE), F32),
                        pltpu.VMEM((n_lt, eb, LANE), BF16), pltpu.VMEM((n_lt, eb, LANE), BF16)],
        compiler_params=_params(("arbitrary",)),
        name="peer_main",
    )(h2t, u_bf, vt_bf, s1x, s2, e1x, e2, thr, x1, mod)


def _final_norm_kernel(x_ref, g_ref, o_ref):
    x = x_ref[...]
    o_ref[...] = x * lax.rsqrt(jnp.mean(x * x, axis=-1, keepdims=True) + EPS) * g_ref[...]


def _final_norm(x, g, row_block0, n_rows):
    tm = 512
    return pl.pallas_call(
        _final_norm_kernel,
        grid=(n_rows // tm,),
        in_specs=[pl.BlockSpec((tm, D), lambda i: (row_block0 + i, 0)), _full((1, D))],
        out_specs=pl.BlockSpec((tm, D), lambda i: (i, 0)),
        out_shape=jax.ShapeDtypeStruct((n_rows, D), F32),
        compiler_params=_params(("arbitrary",)),
        name="final_norm",
    )(x, g)


def _pad_cols(w, n):
    return jnp.pad(w, ((0, 0), (0, n - w.shape[1])))


def _prep_w_in(w):
    pts = np.cumsum(np.array(IN_SPLITS))[:-1].tolist()
    q_lat, ckv, krope, na_q, na_k, na_v, c_q, c_k, c_v, d_q, d_k, d_v = jnp.split(w, pts, axis=-1)
    cols = [na_q, na_k, na_v, c_q, d_q, _pad_cols(q_lat, 512), ckv, c_k, c_v, d_k, d_v, _pad_cols(krope, 128),
            jnp.zeros((D, 128), w.dtype)]
    return jnp.concatenate(cols, axis=-1).astype(BF16)


def _prep_w_uq(w):
    w = w.reshape(MLA_Q_RANK, 4, HD + MLA_ROPE)
    nope = w[:, :, :HD].reshape(MLA_Q_RANK, 4 * HD)
    rope = jnp.pad(w[:, :, HD:], ((0, 0), (0, 0), (0, HD - MLA_ROPE))).reshape(MLA_Q_RANK, 4 * HD)
    return jnp.pad(jnp.concatenate([nope, rope], axis=-1), ((0, 512 - MLA_Q_RANK), (0, 0))).astype(BF16)


def _rope_tables():
    t = np.arange(LAT_S)
    rows, cols = (t // GRID_W).astype(np.float32), (t % GRID_W).astype(np.float32)

    def ang(d):
        n = d // 4
        freqs = np.float32(ROPE_THETA) ** (-np.arange(n, dtype=np.float32) / np.float32(n))
        return np.concatenate([rows[:, None] * freqs, cols[:, None] * freqs], axis=-1).astype(np.float32)

    a = ang(MLA_ROPE)
    c, s, z = np.cos(a), np.sin(a), np.zeros_like(a)
    tab64 = (np.concatenate([c, c, z, z], -1), np.concatenate([-s, z, z, z], -1), np.concatenate([z, s, z, z], -1))
    a = ang(HD)
    c, s, z = np.cos(a), np.sin(a), np.zeros_like(a)
    tab128 = (np.concatenate([c, c], -1), np.concatenate([-s, z], -1), np.concatenate([z, s], -1))
    return tuple(jnp.asarray(t) for t in tab64), tuple(jnp.asarray(t) for t in tab128)


def kernel(x_prompt, x_sample, c, cache_mla_ckv, cache_mla_krope, cache_na_k, cache_na_v, cache_gqa_k,
           cache_gqa_v, cache_swa_k, cache_swa_v, c_ctx, w_mod, b_mod, norm1_g, w_in, mla_q_norm_g, mla_w_uq,
           mla_kv_norm_g, mla_w_uk, mla_w_uv, na_rpb, gqa_q_norm_g, gqa_k_norm_g, swa_sink, w_out, norm2_g,
           peer_w_q, peer_subkeys, peer_u, peer_v, final_norm_g):
    x = jnp.concatenate([x_prompt.reshape(N_CTX, D), x_sample.reshape(N_LAT_B * LAT_S, D)], axis=0)
    cond8 = jnp.concatenate([c_ctx[None, :], c, jnp.zeros((3, D), F32)], axis=0)
    mod_all = _modulation(cond8, w_mod, b_mod).reshape(DEPTH, 8, N_MOD, D)
    tab64, tab128 = _rope_tables()
    ckr_p = jnp.pad(cache_mla_krope, ((0, 0), (0, 0), (0, 0), (0, HD - MLA_ROPE)))
    c_na_k = cache_na_k.reshape(N_LAT_B, DEPTH, PAST, 512)
    c_na_v = cache_na_v.reshape(N_LAT_B, DEPTH, PAST, 512)
    c_gqa_k = cache_gqa_k.reshape(N_LAT_B, DEPTH, PAST, 256)
    c_gqa_v = cache_gqa_v.reshape(N_LAT_B, DEPTH, PAST, 256)
    c_swa_k = cache_swa_k.reshape(N_LAT_B, DEPTH, PAST, 256)
    c_swa_v = cache_swa_v.reshape(N_LAT_B, DEPTH, PAST, 256)

    st_a = st_b = st_c = st_d = (None, None)
    for l in range(DEPTH):
        mod = mod_all[l]
        proj = _inproj(x, mod, norm1_g[l][None, :], _prep_w_in(w_in[l]))
        mla_w = (jnp.pad(mla_q_norm_g[l], (0, 512 - MLA_Q_RANK))[None, :], _prep_w_uq(mla_w_uq[l]),
                 mla_kv_norm_g[l][None, :], mla_w_uk[l].astype(BF16), mla_w_uv[l].astype(BF16))
        gqa_w = (gqa_q_norm_g[l][None, :], gqa_k_norm_g[l][None, :])
        oa, *st_a = _mla(proj, mla_w, l, False, st_a)
        oa, = _mla(proj, mla_w, l, True, (oa,), (cache_mla_ckv, ckr_p), tab64)
        ob, *st_b = _na(proj, l, False, st_b)
        ob, = _na(proj, l, True, (ob,), (c_na_k, c_na_v), _na_bias(na_rpb[l]))
        oc, *st_c = _gqa(proj, gqa_w, l, False, st_c)
        oc, = _gqa(proj, gqa_w, l, True, (oc,), (c_gqa_k, c_gqa_v), tab128)
        od, *st_d = _swa(proj, swa_sink[l], l, False, st_d)
        od, = _swa(proj, swa_sink[l], l, True, (od,), (c_swa_k, c_swa_v), tab128)
        x1, h2t = _outproj((oa, ob, oc, od), x, mod, norm2_g[l][None, :], w_out[l].astype(BF16).reshape(4, 512, D))
        s1, s2, e1, e2, thr = _peer_scores(h2t, peer_w_q[l].T.astype(BF16),
                                           peer_subkeys[l].reshape(2 * PEER_HEADS, N_KEYS, N_KEYS).astype(BF16))
        key_major = lambda t: t.transpose(2, 0, 1, 3).reshape(N_KEYS, PEER_HEADS, N_TOK)
        stats = (key_major(s1), s2, key_major(e1), e2, thr)
        x = _peer_main(h2t, peer_u[l].astype(BF16), peer_v[l].T.astype(BF16), stats, x1, mod)

    g = final_norm_g[None, :]
    y_prompt = _final_norm(x, g, 0, N_CTX).reshape(N_CTX_B, CTX_S, D)
    y_sample = _final_norm(x, g, N_CTX // 512, N_LAT_B * LAT_S).reshape(N_LAT_B, LAT_S, D)
    heads = lambda t, n: t.reshape(N_CTX_B, DEPTH, CTX_S, n, HD)
    return (y_prompt, y_sample, st_a[0], st_a[1], heads(st_b[0], 4), heads(st_b[1], 4),
            heads(st_c[0], 2), heads(st_c[1], 2), heads(st_d[0], 2), heads(st_d[1], 2))
```
